```python
import math
import jax, jax.numpy as jnp
from jax import lax
import numpy as np

D_MODEL = 1024
BATCH = 8
SEQ = 4096
DEPTH = 1

GRID_W = 64
CTX_LEN = 256
N_MOD = 9
FFN_DIM = 2816
RMS_EPS = 1e-6
S5_WIDTH = 512
S5_GROUP = 16
S5_GROUPS = S5_WIDTH // S5_GROUP
S5_STATE = 64
RWKV_WIDTH = 512
RWKV_HEAD = 64
RWKV_HEADS = RWKV_WIDTH // RWKV_HEAD
DECAY_LORA = 64
AAA_LORA = 64
GATE_LORA = 128
LN_X_EPS = 64e-5
CONV_K = 3
IN_COLS = S5_WIDTH + 3 * RWKV_WIDTH + 2 * DECAY_LORA + 2 * AAA_LORA + GATE_LORA + 2 * D_MODEL

kernel_name = 'hybrid_s5_rwkv7_macaron_prefix_dit'


def rmsnorm(x, g):
    xf = x.astype(jnp.float32)
    y = xf * lax.rsqrt(jnp.mean(xf * xf, axis=-1, keepdims=True) + RMS_EPS)
    return (y * g.astype(jnp.float32)).astype(x.dtype)


def modulate(x, g, shift, scale):
    return rmsnorm(x, g) * (1.0 + scale) + shift


def half_ffn(x, mod, j, g, w_gate, w_up, w_down):
    h = modulate(x, g, mod[3 * j], mod[3 * j + 1])
    y = (jax.nn.silu(h @ w_gate) * (h @ w_up)) @ w_down
    return x + 0.5 * mod[3 * j + 2] * y


def centred_conv(x, w, rows):
    b, n, ch = x.shape
    img = x.reshape(b, rows, n // rows, ch)
    out = lax.conv_general_dilated(img, w[:, :, None, :].astype(x.dtype), window_strides=(1, 1), padding='SAME',
                                   dimension_numbers=('NHWC', 'HWIO', 'NHWC'), feature_group_count=ch)
    return out.reshape(b, n, ch)


def _cmul(ar, ai, br, bi):
    return ar * br - ai * bi, ar * bi + ai * br


def _s5_combine(e1, e2):
    a1r, a1i, b1r, b1i = e1
    a2r, a2i, b2r, b2i = e2
    ar, ai = _cmul(a2r, a2i, a1r, a1i)
    br, bi = _cmul(a2r, a2i, b1r, b1i)
    return ar, ai, br + b2r, bi + b2i


def s5_scan(u, lam_re, lam_im, log_dt, b_re, b_im, c_re, c_im, h0_re, h0_im, reverse):
    f32 = jnp.float32
    lam_re = lam_re.astype(f32)
    lam_im = lam_im.astype(f32)
    dt = jnp.exp(log_dt.astype(f32))[:, None]
    mag = jnp.exp(dt * lam_re)
    ab_re = mag * jnp.cos(dt * lam_im)
    ab_im = mag * jnp.sin(dt * lam_im)
    den = lam_re * lam_re + lam_im * lam_im
    z_re = ((ab_re - 1.0) * lam_re + ab_im * lam_im) / den
    z_im = (ab_im * lam_re - (ab_re - 1.0) * lam_im) / den
    bb_re, bb_im = _cmul(z_re[..., None], z_im[..., None], b_re.astype(f32), b_im.astype(f32))
    if reverse:
        u = u[:, ::-1]
    xr = jnp.einsum('blgi,gpi->lbgp', u, bb_re)
    xi = jnp.einsum('blgi,gpi->lbgp', u, bb_im)
    hr, hi = _cmul(ab_re, ab_im, h0_re, h0_im)
    xr = xr.at[0].add(hr)
    xi = xi.at[0].add(hi)
    n = u.shape[1]
    a_re = jnp.broadcast_to(ab_re, (n, 1) + ab_re.shape)
    a_im = jnp.broadcast_to(ab_im, (n, 1) + ab_im.shape)
    _, _, s_re, s_im = lax.associative_scan(_s5_combine, (a_re, a_im, xr, xi), axis=0)
    y = jnp.einsum('lbgp,gip->blgi', s_re, c_re.astype(f32)) - jnp.einsum('lbgp,gip->blgi', s_im, c_im.astype(f32))
    if reverse:
        y = y[:, ::-1]
    return y, s_re[-1], s_im[-1]


def rwkv_step(S, inp):
    r, w, k, v, aa, bb = inp
    sa = jnp.einsum('bhij,bhj->bhi', S, aa)
    S = S * w[:, :, None, :] + sa[..., None] * bb[:, :, None, :] + v[..., None] * k[:, :, None, :]
    return S, jnp.einsum('bhij,bhj->bhi', S, r)


def rwkv_scan(r, w, k, v, aa, bb, S0, reverse):
    xs = (r.transpose(1, 0, 2, 3), w.transpose(1, 0, 2, 3), k.transpose(1, 0, 2, 3),
          v.transpose(1, 0, 2, 3), aa.transpose(1, 0, 2, 3), bb.transpose(1, 0, 2, 3))
    S, ys = lax.scan(rwkv_step, S0, xs, reverse=reverse)
    return ys.transpose(1, 0, 2, 3), S


def zero_states(b):
    zs = jnp.zeros((b, S5_GROUPS, S5_STATE), jnp.float32)
    zr = jnp.zeros((b, RWKV_HEADS, RWKV_HEAD, RWKV_HEAD), jnp.float32)
    return (zs, zs, zs, zs, zr, zr)


def token_mixers(h, rows, init, p, need_out):
    b, n, _ = h.shape
    f32 = jnp.float32
    proj = h @ p['w_in']
    o1 = S5_WIDTH
    o2 = o1 + 3 * RWKV_WIDTH
    o3 = o2 + 2 * DECAY_LORA
    o4 = o3 + 2 * AAA_LORA
    o5 = o4 + GATE_LORA
    o6 = o5 + D_MODEL
    u = proj[..., :o1].astype(f32)
    rkv = proj[..., o1:o2]
    wd = proj[..., o2:o3].astype(f32).reshape(b, n, 2, DECAY_LORA)
    ad = proj[..., o3:o4].astype(f32).reshape(b, n, 2, AAA_LORA)
    gd = proj[..., o4:o5].astype(f32)
    gate_a = proj[..., o5:o6]
    gate_b = proj[..., o6:]

    ug = u.reshape(b, n, S5_GROUPS, S5_GROUP)
    ya_f, sf_re, sf_im = s5_scan(ug, p['s5_A_re'][0], p['s5_A_im'][0], p['s5_log_dt'][0], p['s5_B_re'][0],
                                 p['s5_B_im'][0], p['s5_C_re'][0], p['s5_C_im'][0], init[0], init[1], False)
    ya_b, sb_re, sb_im = s5_scan(ug, p['s5_A_re'][1], p['s5_A_im'][1], p['s5_log_dt'][1], p['s5_B_re'][1],
                                 p['s5_B_im'][1], p['s5_C_re'][1], p['s5_C_im'][1], init[2], init[3], True)

    rkv = centred_conv(rkv, p['rwkv_conv'], rows).astype(f32)
    r, k, v = jnp.split(rkv, 3, axis=-1)
    w_log = p['rwkv_w0'] + jnp.einsum('bldr,drc->bldc', jnp.tanh(wd), p['rwkv_w2'])
    decay = jnp.exp(-jnp.exp(-jax.nn.softplus(-w_log) - 0.5))
    iclr = jax.nn.sigmoid(p['rwkv_a0'] + jnp.einsum('bldr,drc->bldc', ad, p['rwkv_a2']))
    g = jax.nn.sigmoid(gd) @ p['rwkv_g2']

    def heads(t):
        return t.reshape(t.shape[:-1] + (RWKV_HEADS, RWKV_HEAD))

    kk = heads(k * p['rwkv_k_k'])
    kk = kk * lax.rsqrt(jnp.sum(kk * kk, axis=-1, keepdims=True) + 1e-12)
    k_t = heads(k[:, :, None] * (1.0 + (iclr - 1.0) * p['rwkv_k_a']))
    a_h = heads(iclr)
    d_h = heads(decay)
    rh = heads(r)
    vh = heads(v)
    yb_f, S_f = rwkv_scan(rh, d_h[:, :, 0], k_t[:, :, 0], vh, -kk, kk * a_h[:, :, 0], init[4], False)
    yb_b, S_b = rwkv_scan(rh, d_h[:, :, 1], k_t[:, :, 1], vh, -kk, kk * a_h[:, :, 1], init[5], True)
    states = (sf_re, sf_im, sb_re, sb_im, S_f, S_b)
    if not need_out:
        return None, states

    ya = (ya_f + ya_b).reshape(b, n, S5_WIDTH) + p['s5_D'] * u
    ya = jax.nn.gelu(ya)
    ya = ya * jax.nn.sigmoid(ya @ p['s5_w_glu'])

    yb = yb_f + yb_b
    mu = jnp.mean(yb, axis=-1, keepdims=True)
    var = jnp.mean(jnp.square(yb - mu), axis=-1, keepdims=True)
    yb = ((yb - mu) * lax.rsqrt(var + LN_X_EPS)).reshape(b, n, RWKV_WIDTH) * p['rwkv_ln_g'] + p['rwkv_ln_b']
    bonus = jnp.sum(jnp.sum(rh[:, :, None] * k_t * p['rwkv_r_k'], axis=-1, keepdims=True), axis=2) * vh
    yb = yb + bonus.reshape(b, n, RWKV_WIDTH)
    yb = (yb * g) @ p['rwkv_w_o']

    merged = jax.nn.sigmoid(gate_a) * (ya @ p['s5_w_proj']) + jax.nn.sigmoid(gate_b) * yb
    return merged.astype(h.dtype) @ p['w_out'], states


def setup_inputs(seed: int = 0) -> dict:
    key = jax.random.key(seed)
    ks = iter(jax.random.split(key, 48))

    def nrm(shape, scale):
        return scale * jax.random.normal(next(ks), shape, jnp.float32)

    D = D_MODEL
    G, P, GS = S5_GROUPS, S5_STATE, S5_GROUP
    W = RWKV_WIDTH
    x = nrm((BATCH, SEQ, D), 1.0)
    c = nrm((BATCH, D), 1.0)
    ctx = nrm((BATCH, CTX_LEN, D), 1.0)
    c_ctx = nrm((D,), 1.0)
    w_mod = nrm((DEPTH, D, N_MOD * D), 0.5 * D ** -0.5)
    b_mod = nrm((DEPTH, N_MOD * D), 0.01)
    norm_g = 1.0 + nrm((DEPTH, 3, D), 0.02)
    ffn_w_gate = nrm((DEPTH, 2, D, FFN_DIM), D ** -0.5)
    ffn_w_up = nrm((DEPTH, 2, D, FFN_DIM), D ** -0.5)
    ffn_w_down = nrm((DEPTH, 2, FFN_DIM, D), FFN_DIM ** -0.5)
    w_in = nrm((DEPTH, D, IN_COLS), D ** -0.5)
    n_idx = jnp.arange(P, dtype=jnp.float32)
    s5_A_re = -0.5 + nrm((DEPTH, 2, G, P), 0.01)
    s5_A_im = math.pi * n_idx + nrm((DEPTH, 2, G, P), 0.01)
    s5_log_dt = jax.random.uniform(next(ks), (DEPTH, 2, G), jnp.float32, math.log(1e-3), math.log(1e-1))
    s5_B_re = nrm((DEPTH, 2, G, P, GS), (2.0 * GS) ** -0.5)
    s5_B_im = nrm((DEPTH, 2, G, P, GS), (2.0 * GS) ** -0.5)
    s5_C_re = nrm((DEPTH, 2, G, GS, P), P ** -0.5)
    s5_C_im = nrm((DEPTH, 2, G, GS, P), P ** -0.5)
    s5_D = nrm((DEPTH, S5_WIDTH), 1.0)
    s5_w_glu = nrm((DEPTH, S5_WIDTH, S5_WIDTH), S5_WIDTH ** -0.5)
    s5_w_proj = nrm((DEPTH, S5_WIDTH, D), S5_WIDTH ** -0.5)
    rwkv_conv = nrm((DEPTH, CONV_K, CONV_K, 3 * W), 0.1).at[:, 1, 1].add(1.0)
    ramp = jnp.linspace(0.0, 1.0, W, dtype=jnp.float32)
    rwkv_w0 = -6.0 + 7.0 * ramp ** 0.8 + nrm((DEPTH, 2, W), 0.1)
    rwkv_w2 = nrm((DEPTH, 2, DECAY_LORA, W), 0.1 * DECAY_LORA ** -0.5)
    rwkv_a0 = nrm((DEPTH, 2, W), 0.1)
    rwkv_a2 = nrm((DEPTH, 2, AAA_LORA, W), AAA_LORA ** -0.5)
    rwkv_g2 = nrm((DEPTH, GATE_LORA, W), GATE_LORA ** -0.5)
    rwkv_k_k = 0.85 + nrm((DEPTH, W), 0.02)
    rwkv_k_a = 1.0 + nrm((DEPTH, W), 0.02)
    rwkv_r_k = nrm((DEPTH, RWKV_HEADS, RWKV_HEAD), 0.1)
    rwkv_ln_g = 1.0 + nrm((DEPTH, W), 0.02)
    rwkv_ln_b = nrm((DEPTH, W), 0.01)
    rwkv_w_o = nrm((DEPTH, W, D), W ** -0.5)
    w_out = nrm((DEPTH, D, D), D ** -0.5)
    final_g = 1.0 + nrm((D,), 0.02)
    return {'x': x, 'c': c, 'ctx': ctx, 'c_ctx': c_ctx, 'w_mod': w_mod, 'b_mod': b_mod, 'norm_g': norm_g,
            'ffn_w_gate': ffn_w_gate, 'ffn_w_up': ffn_w_up, 'ffn_w_down': ffn_w_down, 'w_in': w_in,
            's5_A_re': s5_A_re, 's5_A_im': s5_A_im, 's5_log_dt': s5_log_dt, 's5_B_re': s5_B_re, 's5_B_im': s5_B_im,
            's5_C_re': s5_C_re, 's5_C_im': s5_C_im, 's5_D': s5_D, 's5_w_glu': s5_w_glu, 's5_w_proj': s5_w_proj,
            'rwkv_conv': rwkv_conv, 'rwkv_w0': rwkv_w0, 'rwkv_w2': rwkv_w2, 'rwkv_a0': rwkv_a0, 'rwkv_a2': rwkv_a2,
            'rwkv_g2': rwkv_g2, 'rwkv_k_k': rwkv_k_k, 'rwkv_k_a': rwkv_k_a, 'rwkv_r_k': rwkv_r_k,
            'rwkv_ln_g': rwkv_ln_g, 'rwkv_ln_b': rwkv_ln_b, 'rwkv_w_o': rwkv_w_o, 'w_out': w_out, 'final_g': final_g}


def reference(x, c, ctx, c_ctx, w_mod, b_mod, norm_g, ffn_w_gate, ffn_w_up, ffn_w_down, w_in,
              s5_A_re, s5_A_im, s5_log_dt, s5_B_re, s5_B_im, s5_C_re, s5_C_im, s5_D, s5_w_glu, s5_w_proj,
              rwkv_conv, rwkv_w0, rwkv_w2, rwkv_a0, rwkv_a2, rwkv_g2, rwkv_k_k, rwkv_k_a, rwkv_r_k,
              rwkv_ln_g, rwkv_ln_b, rwkv_w_o, w_out, final_g):
    b = x.shape[0]
    rows = x.shape[1] // GRID_W
    for l in range(DEPTH):
        need_ctx = l < DEPTH - 1
        mod_x = (jax.nn.silu(c) @ w_mod[l] + b_mod[l]).reshape(b, N_MOD, D_MODEL).transpose(1, 0, 2)[:, :, None, :]
        mod_c = (jax.nn.silu(c_ctx) @ w_mod[l] + b_mod[l]).reshape(N_MOD, 1, 1, D_MODEL)
        p = {'w_in': w_in[l], 's5_A_re': s5_A_re[l], 's5_A_im': s5_A_im[l], 's5_log_dt': s5_log_dt[l],
             's5_B_re': s5_B_re[l], 's5_B_im': s5_B_im[l], 's5_C_re': s5_C_re[l], 's5_C_im': s5_C_im[l],
             's5_D': s5_D[l], 's5_w_glu': s5_w_glu[l], 's5_w_proj': s5_w_proj[l], 'rwkv_conv': rwkv_conv[l],
             'rwkv_w0': rwkv_w0[l], 'rwkv_w2': rwkv_w2[l], 'rwkv_a0': rwkv_a0[l], 'rwkv_a2': rwkv_a2[l],
             'rwkv_g2': rwkv_g2[l], 'rwkv_k_k': rwkv_k_k[l], 'rwkv_k_a': rwkv_k_a[l], 'rwkv_r_k': rwkv_r_k[l],
             'rwkv_ln_g': rwkv_ln_g[l], 'rwkv_ln_b': rwkv_ln_b[l], 'rwkv_w_o': rwkv_w_o[l], 'w_out': w_out[l]}
        x = half_ffn(x, mod_x, 0, norm_g[l, 0], ffn_w_gate[l, 0], ffn_w_up[l, 0], ffn_w_down[l, 0])
        ctx = half_ffn(ctx, mod_c, 0, norm_g[l, 0], ffn_w_gate[l, 0], ffn_w_up[l, 0], ffn_w_down[l, 0])
        hc = modulate(ctx, norm_g[l, 1], mod_c[3], mod_c[4])
        hx = modulate(x, norm_g[l, 1], mod_x[3], mod_x[4])
        out_c, ctx_states = token_mixers(hc, 1, zero_states(ctx.shape[0]), p, need_ctx)
        out_x, _ = token_mixers(hx, rows, ctx_states, p, True)
        x = x + mod_x[5] * out_x
        x = half_ffn(x, mod_x, 2, norm_g[l, 2], ffn_w_gate[l, 1], ffn_w_up[l, 1], ffn_w_down[l, 1])
        if need_ctx:
            ctx = ctx + mod_c[5] * out_c
            ctx = half_ffn(ctx, mod_c, 2, norm_g[l, 2], ffn_w_gate[l, 1], ffn_w_up[l, 1], ffn_w_down[l, 1])
    return rmsnorm(x, final_g)
```

```python
import functools
import math

import jax
import jax.numpy as jnp
from jax import lax
from jax.experimental import pallas as pl
from jax.experimental.pallas import tpu as pltpu

F32 = jnp.float32
BF16 = jnp.bfloat16

D_MODEL = 1024
N_MOD = 9
FFN_DIM = 2816
RMS_EPS = 1e-6
GRID_W = 64
S5_WIDTH = 512
S5_GROUP = 16
S5_GROUPS = 32
S5_STATE = 64
RWKV_WIDTH = 512
RWKV_HEAD = 64
RWKV_HEADS = 8
LN_X_EPS = 64e-5

S5_CHUNK = 16
RWKV_CHUNK = 64
TOKEN_TILE = 512
SMALL_TILE = 256
VMEM_LIMIT = 56 * 1024 * 1024


def _cparams(*sem):
    return pltpu.CompilerParams(dimension_semantics=sem, vmem_limit_bytes=VMEM_LIMIT)


def _resident(shape):
    nd = len(shape)
    return pl.BlockSpec(shape, lambda *_: (0,) * nd, pipeline_mode=pl.Buffered(1))


def _dot(a, b):
    return jnp.dot(a, b, preferred_element_type=F32)


def _dot_nt(a, b):
    return lax.dot_general(a, b, (((1,), (1,)), ((), ())), preferred_element_type=F32)


def _split3(x):
    hi = x.astype(BF16)
    r1 = x - hi.astype(F32)
    mid = r1.astype(BF16)
    lo = (r1 - mid.astype(F32)).astype(BF16)
    return hi, mid, lo


def _dot_left01(m01, x):
    hi, mid, lo = _split3(x)
    return _dot(m01, hi) + _dot(m01, mid) + _dot(m01, lo)


def _dot_right01(x, m01):
    hi, mid, lo = _split3(x)
    return _dot(hi, m01) + _dot(mid, m01) + _dot(lo, m01)


def _rms_mod(x, g, shift, scale):
    y = x * lax.rsqrt(jnp.mean(x * x, axis=-1, keepdims=True) + RMS_EPS)
    return (y * g) * (1.0 + scale) + shift


def _mod_kernel(c_ref, w_ref, b_ref, o_ref):
    c = c_ref[...]
    s = c * jax.nn.sigmoid(c)
    o_ref[...] = jnp.dot(s, w_ref[...], precision=lax.Precision.HIGHEST, preferred_element_type=F32) + b_ref[...]


def _mod_call(cc, w_mod, b_mod):
    rows = cc.shape[0]
    return pl.pallas_call(
        _mod_kernel,
        grid=(N_MOD,),
        in_specs=[pl.BlockSpec((rows, D_MODEL), lambda j: (0, 0)),
                  pl.BlockSpec((D_MODEL, D_MODEL), lambda j: (0, j)),
                  pl.BlockSpec((1, D_MODEL), lambda j: (0, j))],
        out_specs=pl.BlockSpec((rows, D_MODEL), lambda j: (0, j)),
        out_shape=jax.ShapeDtypeStruct((rows, N_MOD * D_MODEL), F32),
        compiler_params=_cparams("arbitrary"),
        name="mod",
    )(cc, w_mod, b_mod)


def _ffn_kernel(x_ref, mod_ref, g_ref, wg_ref, wu_ref, wd_ref, fg_ref, o_ref, *, j, final_norm):
    x = x_ref[...]
    m = mod_ref[0]
    h = _rms_mod(x, g_ref[...], m[3 * j:3 * j + 1], m[3 * j + 1:3 * j + 2]).astype(BF16)
    a = _dot(h, wg_ref[...])
    a = (a * jax.nn.sigmoid(a)) * _dot(h, wu_ref[...])
    y = _dot(a.astype(BF16), wd_ref[...])
    out = x + 0.5 * m[3 * j + 2:3 * j + 3] * y
    if final_norm:
        out = out * lax.rsqrt(jnp.mean(out * out, axis=-1, keepdims=True) + RMS_EPS) * fg_ref[...]
    o_ref[...] = out


def _mod_spec(n_tiles, tiles_per_batch, shared_row):
    if shared_row is not None:
        return pl.BlockSpec((1, N_MOD, D_MODEL), lambda i: (shared_row, 0, 0))
    assert n_tiles % tiles_per_batch == 0
    return pl.BlockSpec((1, N_MOD, D_MODEL), lambda i: (i // tiles_per_batch, 0, 0))


def _ffn_call(x, mods, g, wg, wu, wd, fg, *, j, tokens_per_batch, shared_row, final_norm):
    n = x.shape[0]
    tm = min(TOKEN_TILE, tokens_per_batch)
    tok = pl.BlockSpec((tm, D_MODEL), lambda i: (i, 0))
    return pl.pallas_call(
        functools.partial(_ffn_kernel, j=j, final_norm=final_norm),
        grid=(n // tm,),
        in_specs=[tok, _mod_spec(n // tm, tokens_per_batch // tm, shared_row), _resident((1, D_MODEL)),
                  _resident((D_MODEL, FFN_DIM)), _resident((D_MODEL, FFN_DIM)), _resident((FFN_DIM, D_MODEL)),
                  _resident((1, D_MODEL))],
        out_specs=tok,
        out_shape=jax.ShapeDtypeStruct((n, D_MODEL), F32),
        compiler_params=_cparams("parallel"),
        name="ffn",
    )(x, mods, g, wg, wu, wd, fg)


O_U = S5_WIDTH
O_RKV = O_U + 3 * RWKV_WIDTH
O_WD = O_RKV + 128
O_AD = O_WD + 128
O_GD = O_AD + 128
O_GA = O_GD + D_MODEL
IN_COLS = O_GA + D_MODEL
DECAY_SCALE = math.exp(-0.5)


def _inproj_kernel(x_ref, mod_ref, g_ref, w_ref, w2_ref, w0_ref, a2_ref, a0_ref, g2_ref,
                   u_ref, rkv_ref, lw_ref, ic_ref, gg_ref, sga_ref, sgb_ref):
    m = mod_ref[0]
    h = _rms_mod(x_ref[...], g_ref[...], m[3:4], m[4:5]).astype(BF16)
    proj = _dot(h, w_ref[...])
    u_ref[...] = proj[:, :O_U]
    rkv_ref[...] = proj[:, O_U:O_RKV]
    w_log = w0_ref[...] + _dot(jnp.tanh(proj[:, O_RKV:O_WD]).astype(BF16), w2_ref[...])
    lw = -DECAY_SCALE * jax.nn.sigmoid(w_log)
    ic = jax.nn.sigmoid(a0_ref[...] + _dot(proj[:, O_WD:O_AD].astype(BF16), a2_ref[...]))
    for d in range(2):
        lw_ref[d] = lw[:, d * RWKV_WIDTH:(d + 1) * RWKV_WIDTH]
        ic_ref[d] = ic[:, d * RWKV_WIDTH:(d + 1) * RWKV_WIDTH]
    gg_ref[...] = _dot(jax.nn.sigmoid(proj[:, O_AD:O_GD]).astype(BF16), g2_ref[...])
    sga_ref[...] = jax.nn.sigmoid(proj[:, O_GD:O_GA])
    sgb_ref[...] = jax.nn.sigmoid(proj[:, O_GA:])


def _inproj_call(x, mods, g, w_in, w2c, w0c, a2c, a0c, g2, *, tokens_per_batch, shared_row):
    n = x.shape[0]
    tm = min(SMALL_TILE, tokens_per_batch)
    w = RWKV_WIDTH

    def tok(width):
        return pl.BlockSpec((tm, width), lambda i: (i, 0))

    def tok2():
        return pl.BlockSpec((2, tm, w), lambda i: (0, i, 0))

    def shp(width):
        return jax.ShapeDtypeStruct((n, width), F32)

    shp2 = jax.ShapeDtypeStruct((2, n, w), F32)
    return pl.pallas_call(
        _inproj_kernel,
        grid=(n // tm,),
        in_specs=[tok(D_MODEL), _mod_spec(n // tm, tokens_per_batch // tm, shared_row), _resident((1, D_MODEL)),
                  _resident((D_MODEL, IN_COLS)), _resident((128, 2 * RWKV_WIDTH)), _resident((1, 2 * RWKV_WIDTH)),
                  _resident((128, 2 * RWKV_WIDTH)), _resident((1, 2 * RWKV_WIDTH)), _resident((128, RWKV_WIDTH))],
        out_specs=[tok(S5_WIDTH), tok(3 * w), tok2(), tok2(), tok(w), tok(D_MODEL), tok(D_MODEL)],
        out_shape=[shp(S5_WIDTH), shp(3 * w), shp2, shp2, shp(w), shp(D_MODEL), shp(D_MODEL)],
        compiler_params=_cparams("parallel"),
        name="inproj",
    )(x, mods, g, w_in, w2c, w0c, a2c, a0c, g2)


def _prep_kernel(up_ref, mid_ref, dn_ref, ic_ref, cw_ref, kk_w_ref, ka_ref, rk_ref, seg_ref,
                 r_ref, v_ref, kk_ref, k2_ref, bonus_ref, *, width, vertical, tiles_per_image):
    tm = mid_ref.shape[0]
    cw = cw_ref[...]
    mid = mid_ref[...]
    xpos = lax.broadcasted_iota(jnp.int32, (tm, 1), 0) % width
    not_first = xpos != 0
    not_last = xpos != width - 1

    def row_taps(base, dy):
        left = jnp.where(not_first, pltpu.roll(base, 1, 0), 0.0)
        right = jnp.where(not_last, pltpu.roll(base, tm - 1, 0), 0.0)
        return left * cw[3 * dy:3 * dy + 1] + base * cw[3 * dy + 1:3 * dy + 2] + right * cw[3 * dy + 2:3 * dy + 3]

    acc = row_taps(mid, 1)
    if vertical:
        t = pl.program_id(0) % tiles_per_image
        up = jnp.where(t == 0, 0.0, up_ref[...])
        dn = jnp.where(t == tiles_per_image - 1, 0.0, dn_ref[...])
        ext = jnp.concatenate([up, mid, dn], axis=0)
        acc = acc + row_taps(ext[0:tm], 0) + row_taps(ext[2 * width:2 * width + tm], 2)

    w = RWKV_WIDTH
    r, k, v = acc[:, :w], acc[:, w:2 * w], acc[:, 2 * w:]
    seg = seg_ref[...]
    kk = k * kk_w_ref[...]
    kk = kk * lax.rsqrt(_dot_right01(kk * kk, seg) + 1e-12)
    ka = ka_ref[...]
    rrk = r * rk_ref[...]
    kf = k * (1.0 + (ic_ref[0] - 1.0) * ka)
    kb = k * (1.0 + (ic_ref[1] - 1.0) * ka)
    r_ref[...] = r
    v_ref[...] = v
    kk_ref[...] = kk
    k2_ref[0] = kf
    k2_ref[1] = kb
    bonus_ref[...] = _dot_right01(rrk * kf + rrk * kb, seg) * v


def _prep_call(rkv, iclr, cw, kk_w, ka, rk, seg, *, tokens_per_batch, width, vertical):
    n = rkv.shape[0]
    tm = min(SMALL_TILE, tokens_per_batch)
    rows_per_tile = tm // width
    n_rows = n // width
    w3 = 3 * RWKV_WIDTH
    w = RWKV_WIDTH
    tok = pl.BlockSpec((tm, w), lambda i: (i, 0))
    return pl.pallas_call(
        functools.partial(_prep_kernel, width=width, vertical=vertical, tiles_per_image=tokens_per_batch // tm),
        grid=(n // tm,),
        in_specs=[pl.BlockSpec((width, w3), lambda i: (jnp.maximum(i * rows_per_tile - 1, 0), 0)),
                  pl.BlockSpec((tm, w3), lambda i: (i, 0)),
                  pl.BlockSpec((width, w3), lambda i: (jnp.minimum((i + 1) * rows_per_tile, n_rows - 1), 0)),
                  pl.BlockSpec((2, tm, w), lambda i: (0, i, 0)),
                  _resident((9, w3)), _resident((1, w)), _resident((1, w)), _resident((1, w)), _resident((w, w))],
        out_specs=[tok, tok, tok, pl.BlockSpec((2, tm, w), lambda i: (0, i, 0)), tok],
        out_shape=[jax.ShapeDtypeStruct((n, w), F32)] * 3 + [jax.ShapeDtypeStruct((2, n, w), F32),
                                                              jax.ShapeDtypeStruct((n, w), F32)],
        compiler_params=_cparams("parallel"),
        name="prep",
    )(rkv, rkv, rkv, iclr, cw, kk_w, ka, rk, seg)


def _s5_kernel(u_ref, wcat_ref, tfb_ref, cp_ref, ab_ref, y_ref, x_scr, h_scr, *, ctx_chunks, x_chunks, nb):
    u = u_ref[0]
    x_scr[...] = _dot(u, wcat_ref[0])
    ab = ab_ref[0]
    total = ctx_chunks + x_chunks

    def scan(carry, first, count, step, a1, a2, col):
        def body(i, hc):
            h, hs = hc
            off = pl.multiple_of((first + i * step) * nb, nb)
            h_scr[pl.ds(off, nb), col * 128:(col + 1) * 128] = h
            x = x_scr[pl.ds(off, nb), 2 * col * 128:(2 * col + 1) * 128]
            xs = x_scr[pl.ds(off, nb), (2 * col + 1) * 128:(2 * col + 2) * 128]
            return a1 * h + a2 * hs + x, a1 * hs - a2 * h + xs
        return lax.fori_loop(0, count, body, carry)

    zero = jnp.zeros((nb, 128), F32)
    scan((zero, zero), 0, total, 1, ab[0], ab[1], 0)
    carry = scan((zero, zero), ctx_chunks - 1, ctx_chunks, -1, ab[2], ab[3], 1)
    scan(carry, total - 1, x_chunks, -1, ab[2], ab[3], 1)

    r0 = ctx_chunks * nb
    y_ref[0] = _dot(u[r0:], tfb_ref[0]) + _dot(h_scr[r0:, :].astype(BF16), cp_ref[0])


def _s5_call(u_t, wcat, tfb, cp, ab, *, ctx_chunks, x_chunks, nb):
    g = u_t.shape[0]
    rows = (ctx_chunks + x_chunks) * nb
    lw = S5_CHUNK * S5_GROUP
    return pl.pallas_call(
        functools.partial(_s5_kernel, ctx_chunks=ctx_chunks, x_chunks=x_chunks, nb=nb),
        grid=(g,),
        in_specs=[pl.BlockSpec((1, rows, lw), lambda i: (i, 0, 0)),
                  pl.BlockSpec((1, lw, 512), lambda i: (i, 0, 0)),
                  pl.BlockSpec((1, lw, lw), lambda i: (i, 0, 0)),
                  pl.BlockSpec((1, 256, lw), lambda i: (i, 0, 0)),
                  pl.BlockSpec((1, 4, nb, 128), lambda i: (i, 0, 0, 0))],
        out_specs=pl.BlockSpec((1, x_chunks * nb, lw), lambda i: (i, 0, 0)),
        out_shape=jax.ShapeDtypeStruct((g, x_chunks * nb, lw), F32),
        scratch_shapes=[pltpu.VMEM((rows, 512), F32), pltpu.VMEM((rows, 256), F32)],
        compiler_params=_cparams("parallel"),
        name="s5",
    )(u_t, wcat, tfb, cp, ab)


def _s5_matrices(lam_re, lam_im, log_dt, b_re, b_im, c_re, c_im, nb):
    L, P, GS = S5_CHUNK, S5_STATE, S5_GROUP
    hp = lax.Precision.HIGHEST
    dt = jnp.exp(log_dt)[..., None]
    ab_re = jnp.exp(dt * lam_re) * jnp.cos(dt * lam_im)
    ab_im = jnp.exp(dt * lam_re) * jnp.sin(dt * lam_im)
    den = lam_re * lam_re + lam_im * lam_im
    z_re = ((ab_re - 1.0) * lam_re + ab_im * lam_im) / den
    z_im = (ab_im * lam_re - (ab_re - 1.0) * lam_im) / den
    bb_re = z_re[..., None] * b_re - z_im[..., None] * b_im
    bb_im = z_re[..., None] * b_im + z_im[..., None] * b_re
    k = jnp.arange(L + 1, dtype=F32)[:, None, None, None]
    pw_re = jnp.exp(k * dt * lam_re) * jnp.cos(k * dt * lam_im)
    pw_im = jnp.exp(k * dt * lam_re) * jnp.sin(k * dt * lam_im)
    ca_re = c_re[None] * pw_re[:, :, :, None, :] - c_im[None] * pw_im[:, :, :, None, :]
    ca_im = c_re[None] * pw_im[:, :, :, None, :] + c_im[None] * pw_re[:, :, :, None, :]
    ab_re_k = pw_re[..., None] * bb_re[None] - pw_im[..., None] * bb_im[None]
    ab_im_k = pw_re[..., None] * bb_im[None] + pw_im[..., None] * bb_re[None]
    kern = (jnp.einsum('kdgip,dgpj->dgkij', ca_re[:L], bb_re, precision=hp)
            - jnp.einsum('kdgip,dgpj->dgkij', ca_im[:L], bb_im, precision=hp))

    s = jnp.arange(L)[:, None]
    t = jnp.arange(L)[None, :]

    def toeplitz(kd, lag, valid):
        m = jnp.where(valid[None, :, :, None, None], kd[:, jnp.clip(lag, 0, L - 1)], 0.0)
        return m.transpose(0, 1, 4, 2, 3).reshape(-1, L * GS, L * GS)

    tfb = toeplitz(kern[0], t - s, t >= s) + toeplitz(kern[1], s - t, s >= t)

    def state_in(d, powers):
        wr = ab_re_k[powers, d].transpose(1, 0, 3, 2).reshape(-1, L * GS, P)
        wi = ab_im_k[powers, d].transpose(1, 0, 3, 2).reshape(-1, L * GS, P)
        return jnp.concatenate([wr, wi, wi, wr], axis=-1)

    wcat = jnp.concatenate([state_in(0, L - 1 - jnp.arange(L)), state_in(1, jnp.arange(L))], axis=-1)

    def state_out(d, powers):
        cr = ca_re[powers, d].transpose(1, 3, 0, 2).reshape(-1, P, L * GS)
        ci = ca_im[powers, d].transpose(1, 3, 0, 2).reshape(-1, P, L * GS)
        return jnp.concatenate([cr, -ci], axis=1)

    cp = jnp.concatenate([state_out(0, 1 + jnp.arange(L)), state_out(1, L - jnp.arange(L))], axis=1)

    def step_coeffs(d):
        a1 = jnp.concatenate([pw_re[L, d], pw_re[L, d]], axis=-1)
        a2 = jnp.concatenate([-pw_im[L, d], pw_im[L, d]], axis=-1)
        return [a1, a2]

    ab = jnp.stack(step_coeffs(0) + step_coeffs(1), axis=1)
    ab = jnp.broadcast_to(ab[:, :, None, :], ab.shape[:2] + (nb, 128))
    return wcat.astype(BF16), tfb.astype(BF16), cp.astype(BF16), ab


def _rwkv_kernel(r_ref, v_ref, kk_ref, k_ref, lw_ref, ic_ref, s0_ref, y_ref, sT_ref, s_scr):
    d = pl.program_id(0)
    c = pl.program_id(2)
    L, N = RWKV_CHUNK, RWKV_HEAD

    @pl.when(c == 0)
    def _():
        s_scr[...] = s0_ref[0, 0]

    row = lax.broadcasted_iota(jnp.int32, (L, L), 0)
    col = lax.broadcasted_iota(jnp.int32, (L, L), 1)
    ahead = (row - col) * (1 - 2 * d)
    strict = ahead > 0
    incl = ahead >= 0
    eye = (col == row).astype(F32)

    lw = lw_ref[0]
    cum = _dot_left01(jnp.where(incl, 1.0, 0.0).astype(BF16), lw)
    tot = jnp.sum(lw, axis=0, keepdims=True)
    r, v, kk, k, ic = r_ref[...], v_ref[...], kk_ref[...], k_ref[0], ic_ref[0]
    b = kk * ic
    e_ninc = jnp.exp(-cum)
    e_rem = jnp.exp(tot - cum)
    a_t = -kk * jnp.exp(cum - lw)
    r_t = r * jnp.exp(cum)
    b_t = (b * e_ninc).astype(BF16)
    k_t = (k * e_ninc).astype(BF16)
    b_h = (b * e_rem).astype(BF16)
    k_h = (k * e_rem).astype(BF16)
    w_tot = jnp.exp(tot)

    for h in range(RWKV_HEADS):
        sl = slice(h * N, (h + 1) * N)
        a_th, r_th, vh = a_t[:, sl], r_t[:, sl], v[:, sl]
        ar = jnp.concatenate([a_th, r_th], axis=0).astype(BF16)
        g1 = _dot_nt(ar, b_t[:, sl])
        g2 = _dot_nt(ar, k_t[:, sl])
        n_ab = jnp.where(strict, g1[:L], 0.0)
        a_ak = jnp.where(strict, g2[:L], 0.0).astype(BF16)
        a_rb = jnp.where(incl, g1[L:], 0.0).astype(BF16)
        a_rk = jnp.where(incl, g2[L:], 0.0).astype(BF16)
        inv = eye + n_ab
        pw = n_ab
        for _ in range(5):
            pwb = pw.astype(BF16)
            pw = _dot(pwb, pwb)
            inv = inv + _dot(inv.astype(BF16), pw.astype(BF16))
        invb = inv.astype(BF16)
        vhb = vh.astype(BF16)
        a_p = _dot(invb, a_th.astype(BF16))
        v_p = _dot(invb, _dot(a_ak, vhb).astype(BF16))
        a_pb, v_pb = a_p.astype(BF16), v_p.astype(BF16)
        p_lr = _dot(a_p.T.astype(BF16), b_h[:, sl])
        q = _dot(v_p.T.astype(BF16), b_h[:, sl]) + _dot(vh.T.astype(BF16), k_h[:, sl])
        r_p = r_th + _dot(a_rb, a_pb)
        y_loc = _dot(a_rb, v_pb) + _dot(a_rk, vhb)

        s = s_scr[h]
        sb = s.astype(BF16)
        y_ref[0, :, sl] = _dot_nt(r_p.astype(BF16), sb) + y_loc
        s_scr[h] = s * w_tot[:, sl] + _dot(sb, p_lr.astype(BF16)) + q

    @pl.when(c == pl.num_programs(2) - 1)
    def _():
        sT_ref[0, 0] = s_scr[...]


def _rwkv_call(r, v, kk, k2, lw2, ic2, s0, *, tokens_per_batch):
    n = r.shape[0]
    nb = n // tokens_per_batch
    nc = tokens_per_batch // RWKV_CHUNK
    w = RWKV_WIDTH

    def blk(d, b, c):
        return b * nc + c + d * (nc - 1 - 2 * c)

    tok = pl.BlockSpec((RWKV_CHUNK, w), lambda d, b, c: (blk(d, b, c), 0))
    tok2 = pl.BlockSpec((1, RWKV_CHUNK, w), lambda d, b, c: (d, blk(d, b, c), 0))
    st = pl.BlockSpec((1, 1, RWKV_HEADS, RWKV_HEAD, RWKV_HEAD), lambda d, b, c: (d, b, 0, 0, 0))
    return pl.pallas_call(
        _rwkv_kernel,
        grid=(2, nb, nc),
        in_specs=[tok, tok, tok, tok2, tok2, tok2, st],
        out_specs=[tok2, st],
        out_shape=[jax.ShapeDtypeStruct((2, n, w), F32),
                   jax.ShapeDtypeStruct((2, nb, RWKV_HEADS, RWKV_HEAD, RWKV_HEAD), F32)],
        scratch_shapes=[pltpu.VMEM((RWKV_HEADS, RWKV_HEAD, RWKV_HEAD), F32)],
        compiler_params=_cparams("parallel", "parallel", "arbitrary"),
        name="rwkv",
    )(r, v, kk, k2, lw2, ic2, s0)


def _out_kernel(x_ref, mod_ref, ya_ref, u_ref, yb_ref, bonus_ref, gg_ref, sga_ref, sgb_ref,
                s5d_ref, wglu_ref, wproj_ref, lng_ref, lnb_ref, seg_ref, wo_ref, wout_ref, o_ref):
    m = mod_ref[0]
    ya = jax.nn.gelu(ya_ref[...] + s5d_ref[...] * u_ref[...])
    ya = ya * jax.nn.sigmoid(_dot(ya.astype(BF16), wglu_ref[...]))
    pa = _dot(ya.astype(BF16), wproj_ref[...])

    seg = seg_ref[...]
    yb = yb_ref[0] + yb_ref[1]
    mu = _dot_right01(yb, seg) * (1.0 / RWKV_HEAD)
    cen = yb - mu
    var = _dot_right01(cen * cen, seg) * (1.0 / RWKV_HEAD)
    yb = cen * lax.rsqrt(var + LN_X_EPS) * lng_ref[...] + lnb_ref[...] + bonus_ref[...]
    pb = _dot((yb * gg_ref[...]).astype(BF16), wo_ref[...])

    merged = sga_ref[...] * pa + sgb_ref[...] * pb
    o_ref[...] = x_ref[...] + m[5:6] * _dot(merged.astype(BF16), wout_ref[...])


def _out_call(x, mods, ya, u, yb2, bonus, gg, sga, sgb, s5d, wglu, wproj, lng, lnb, seg, wo, wout,
              *, tokens_per_batch):
    n = x.shape[0]
    tm = min(SMALL_TILE, tokens_per_batch)
    w = RWKV_WIDTH

    def tok(width):
        return pl.BlockSpec((tm, width), lambda i: (i, 0))

    return pl.pallas_call(
        _out_kernel,
        grid=(n // tm,),
        in_specs=[tok(D_MODEL), _mod_spec(n // tm, tokens_per_batch // tm, None), tok(w), tok(w),
                  pl.BlockSpec((2, tm, w), lambda i: (0, i, 0)), tok(w), tok(w), tok(D_MODEL), tok(D_MODEL),
                  _resident((1, w)), _resident((w, w)), _resident((w, D_MODEL)), _resident((1, w)),
                  _resident((1, w)), _resident((w, w)), _resident((w, D_MODEL)), _resident((D_MODEL, D_MODEL))],
        out_specs=tok(D_MODEL),
        out_shape=jax.ShapeDtypeStruct((n, D_MODEL), F32),
        compiler_params=_cparams("parallel"),
        name="outk",
    )(x, mods, ya, u, yb2, bonus, gg, sga, sgb, s5d, wglu, wproj, lng, lnb, seg, wo, wout)


def _block_diag2(a):
    z = jnp.zeros_like(a[0])
    return jnp.concatenate([jnp.concatenate([a[0], z], axis=1), jnp.concatenate([z, a[1]], axis=1)], axis=0)


def _to_s5_layout(u, nb):
    n = u.shape[0] // nb
    x = u.reshape(nb, n // S5_CHUNK, S5_CHUNK, S5_GROUPS, S5_GROUP).transpose(3, 1, 0, 2, 4)
    return x.reshape(S5_GROUPS, (n // S5_CHUNK) * nb, S5_CHUNK * S5_GROUP).astype(BF16)


def _from_s5_layout(y, nb):
    nc = y.shape[1] // nb
    x = y.reshape(S5_GROUPS, nc, nb, S5_CHUNK, S5_GROUP).transpose(2, 1, 3, 0, 4)
    return x.reshape(nb * nc * S5_CHUNK, S5_WIDTH)


def kernel(x, c, ctx, c_ctx, w_mod, b_mod, norm_g, ffn_w_gate, ffn_w_up, ffn_w_down, w_in, s5_A_re, s5_A_im, s5_log_dt, s5_B_re, s5_B_im, s5_C_re, s5_C_im, s5_D, s5_w_glu, s5_w_proj, rwkv_conv, rwkv_w0, rwkv_w2, rwkv_a0, rwkv_a2, rwkv_g2, rwkv_k_k, rwkv_k_a, rwkv_r_k, rwkv_ln_g, rwkv_ln_b, rwkv_w_o, w_out, final_g):
    nb, seq, d = x.shape
    n_ctx = ctx.shape[1]
    l = 0
    w = RWKV_WIDTH

    mod_rows = 16
    cc = jnp.concatenate([c, c_ctx[None], jnp.zeros((mod_rows - nb - 1, d), F32)], axis=0)
    mods = _mod_call(cc, w_mod[l], b_mod[l][None]).reshape(mod_rows, N_MOD, d)

    xt = x.reshape(nb * seq, d)
    ct = ctx.reshape(nb * n_ctx, d)
    row = lambda a: a.reshape(1, -1)
    bf = lambda a: a.astype(BF16)

    f1 = (row(norm_g[l, 0]), bf(ffn_w_gate[l, 0]), bf(ffn_w_up[l, 0]), bf(ffn_w_down[l, 0]), row(final_g))
    x1 = _ffn_call(xt, mods, *f1, j=0, tokens_per_batch=seq, shared_row=None, final_norm=False)
    c1 = _ffn_call(ct, mods, *f1, j=0, tokens_per_batch=n_ctx, shared_row=nb, final_norm=False)

    ip = (row(norm_g[l, 1]), bf(w_in[l]), bf(_block_diag2(rwkv_w2[l])), row(rwkv_w0[l]),
          bf(_block_diag2(rwkv_a2[l])), row(rwkv_a0[l]), bf(rwkv_g2[l]))
    u_x, rkv_x, lw_x, ic_x, gg_x, sga_x, sgb_x = _inproj_call(x1, mods, *ip, tokens_per_batch=seq, shared_row=None)
    u_c, rkv_c, lw_c, ic_c, _, _, _ = _inproj_call(c1, mods, *ip, tokens_per_batch=n_ctx, shared_row=nb)

    s5p = _s5_matrices(s5_A_re[l], s5_A_im[l], s5_log_dt[l], s5_B_re[l], s5_B_im[l], s5_C_re[l], s5_C_im[l], nb)
    u_t = jnp.concatenate([_to_s5_layout(u_c, nb), _to_s5_layout(u_x, nb)], axis=1)
    ya = _from_s5_layout(_s5_call(u_t, *s5p, ctx_chunks=n_ctx // S5_CHUNK, x_chunks=seq // S5_CHUNK, nb=nb), nb)

    seg = (jnp.arange(w)[:, None] // RWKV_HEAD == jnp.arange(w)[None, :] // RWKV_HEAD).astype(BF16)
    pp = (rwkv_conv[l].reshape(9, 3 * w), row(rwkv_k_k[l]), row(rwkv_k_a[l]), row(rwkv_r_k[l]), seg)
    r_c, v_c, kk_c, k2_c, _ = _prep_call(rkv_c, ic_c, *pp, tokens_per_batch=n_ctx, width=n_ctx, vertical=False)
    r_x, v_x, kk_x, k2_x, bonus = _prep_call(rkv_x, ic_x, *pp, tokens_per_batch=seq, width=GRID_W, vertical=True)

    s0 = jnp.zeros((2, nb, RWKV_HEADS, RWKV_HEAD, RWKV_HEAD), F32)
    _, s_ctx = _rwkv_call(r_c, v_c, kk_c, k2_c, lw_c, ic_c, s0, tokens_per_batch=n_ctx)
    yb2, _ = _rwkv_call(r_x, v_x, kk_x, k2_x, lw_x, ic_x, s_ctx, tokens_per_batch=seq)

    x2 = _out_call(x1, mods, ya, u_x, yb2, bonus, gg_x, sga_x, sgb_x, row(s5_D[l]), bf(s5_w_glu[l]),
                   bf(s5_w_proj[l]), row(rwkv_ln_g[l]), row(rwkv_ln_b[l]), seg, bf(rwkv_w_o[l]), bf(w_out[l]),
                   tokens_per_batch=seq)

    f2 = (row(norm_g[l, 2]), bf(ffn_w_gate[l, 1]), bf(ffn_w_up[l, 1]), bf(ffn_w_down[l, 1]), row(final_g))
    out = _ffn_call(x2, mods, *f2, j=2, tokens_per_batch=seq, shared_row=None, final_norm=True)
    return out.reshape(nb, seq, d)
```

```python
import functools
import math

import jax
import jax.numpy as jnp
from jax import lax
from jax.experimental import pallas as pl
from jax.experimental.pallas import tpu as pltpu

F32 = jnp.float32
BF16 = jnp.bfloat16

D_MODEL = 1024
N_MOD = 9
FFN_DIM = 2816
RMS_EPS = 1e-6
GRID_W = 64
S5_WIDTH = 512
S5_GROUP = 16
S5_GROUPS = 32
S5_STATE = 64
RWKV_WIDTH = 512
RWKV_HEAD = 64
RWKV_HEADS = 8
LN_X_EPS = 64e-5

S5_CHUNK = 16
RWKV_CHUNK = 64
TOKEN_TILE = 512
SMALL_TILE = 256
VMEM_LIMIT = 56 * 1024 * 1024


def _cparams(*sem):
    return pltpu.CompilerParams(dimension_semantics=sem, vmem_limit_bytes=VMEM_LIMIT)


def _resident(shape):
    nd = len(shape)
    return pl.BlockSpec(shape, lambda *_: (0,) * nd, pipeline_mode=pl.Buffered(1))


def _dot(a, b):
    return jnp.dot(a, b, preferred_element_type=F32)


def _dot_nt(a, b):
    return lax.dot_general(a, b, (((1,), (1,)), ((), ())), preferred_element_type=F32)


def _split3(x):
    hi = x.astype(BF16)
    r1 = x - hi.astype(F32)
    mid = r1.astype(BF16)
    lo = (r1 - mid.astype(F32)).astype(BF16)
    return hi, mid, lo


def _dot_left01(m01, x):
    hi, mid, lo = _split3(x)
    return _dot(m01, hi) + _dot(m01, mid) + _dot(m01, lo)


def _dot_right01(x, m01):
    hi, mid, lo = _split3(x)
    return _dot(hi, m01) + _dot(mid, m01) + _dot(lo, m01)


def _rms_mod(x, g, shift, scale):
    y = x * lax.rsqrt(jnp.mean(x * x, axis=-1, keepdims=True) + RMS_EPS)
    return (y * g) * (1.0 + scale) + shift


def _mod_kernel(c_ref, w_ref, b_ref, o_ref):
    c = c_ref[...]
    s = c * jax.nn.sigmoid(c)
    o_ref[...] = jnp.dot(s, w_ref[...], precision=lax.Precision.HIGHEST, preferred_element_type=F32) + b_ref[...]


def _mod_call(cc, w_mod, b_mod):
    rows = cc.shape[0]
    return pl.pallas_call(
        _mod_kernel,
        grid=(N_MOD,),
        in_specs=[pl.BlockSpec((rows, D_MODEL), lambda j: (0, 0)),
                  pl.BlockSpec((D_MODEL, D_MODEL), lambda j: (0, j)),
                  pl.BlockSpec((1, D_MODEL), lambda j: (0, j))],
        out_specs=pl.BlockSpec((rows, D_MODEL), lambda j: (0, j)),
        out_shape=jax.ShapeDtypeStruct((rows, N_MOD * D_MODEL), F32),
        compiler_params=_cparams("arbitrary"),
        name="mod",
    )(cc, w_mod, b_mod)


def _ffn_kernel(x_ref, mod_ref, g_ref, wg_ref, wu_ref, wd_ref, fg_ref, o_ref, *, j, final_norm):
    x = x_ref[...]
    m = mod_ref[0]
    h = _rms_mod(x, g_ref[...], m[3 * j:3 * j + 1], m[3 * j + 1:3 * j + 2]).astype(BF16)
    a = _dot(h, wg_ref[...])
    a = (a * jax.nn.sigmoid(a)) * _dot(h, wu_ref[...])
    y = _dot(a.astype(BF16), wd_ref[...])
    out = x + 0.5 * m[3 * j + 2:3 * j + 3] * y
    if final_norm:
        out = out * lax.rsqrt(jnp.mean(out * out, axis=-1, keepdims=True) + RMS_EPS) * fg_ref[...]
    o_ref[...] = out


def _mod_spec(n_tiles, tiles_per_batch, shared_row):
    if shared_row is not None:
        return pl.BlockSpec((1, N_MOD, D_MODEL), lambda i: (shared_row, 0, 0))
    assert n_tiles % tiles_per_batch == 0
    return pl.BlockSpec((1, N_MOD, D_MODEL), lambda i: (i // tiles_per_batch, 0, 0))


def _ffn_call(x, mods, g, wg, wu, wd, fg, *, j, tokens_per_batch, shared_row, final_norm):
    n = x.shape[0]
    tm = min(TOKEN_TILE, tokens_per_batch)
    tok = pl.BlockSpec((tm, D_MODEL), lambda i: (i, 0))
    return pl.pallas_call(
        functools.partial(_ffn_kernel, j=j, final_norm=final_norm),
        grid=(n // tm,),
        in_specs=[tok, _mod_spec(n // tm, tokens_per_batch // tm, shared_row), _resident((1, D_MODEL)),
                  _resident((D_MODEL, FFN_DIM)), _resident((D_MODEL, FFN_DIM)), _resident((FFN_DIM, D_MODEL)),
                  _resident((1, D_MODEL))],
        out_specs=tok,
        out_shape=jax.ShapeDtypeStruct((n, D_MODEL), F32),
        compiler_params=_cparams("parallel"),
        name="ffn",
    )(x, mods, g, wg, wu, wd, fg)


O_U = S5_WIDTH
O_RKV = O_U + 3 * RWKV_WIDTH
O_WD = O_RKV + 128
O_AD = O_WD + 128
O_GD = O_AD + 128
O_GA = O_GD + D_MODEL
IN_COLS = O_GA + D_MODEL
DECAY_SCALE = math.exp(-0.5)


def _inproj_kernel(x_ref, mod_ref, g_ref, w_ref, w2_ref, w0_ref, a2_ref, a0_ref, g2_ref,
                   u_ref, rkv_ref, lw_ref, ic_ref, gg_ref, sga_ref, sgb_ref):
    m = mod_ref[0]
    h = _rms_mod(x_ref[...], g_ref[...], m[3:4], m[4:5]).astype(BF16)
    proj = _dot(h, w_ref[...])
    u_ref[...] = proj[:, :O_U]
    rkv_ref[...] = proj[:, O_U:O_RKV]
    w_log = w0_ref[...] + _dot(jnp.tanh(proj[:, O_RKV:O_WD]).astype(BF16), w2_ref[...])
    lw = -DECAY_SCALE * jax.nn.sigmoid(w_log)
    ic = jax.nn.sigmoid(a0_ref[...] + _dot(proj[:, O_WD:O_AD].astype(BF16), a2_ref[...]))
    for d in range(2):
        lw_ref[d] = lw[:, d * RWKV_WIDTH:(d + 1) * RWKV_WIDTH]
        ic_ref[d] = ic[:, d * RWKV_WIDTH:(d + 1) * RWKV_WIDTH]
    gg_ref[...] = _dot(jax.nn.sigmoid(proj[:, O_AD:O_GD]).astype(BF16), g2_ref[...])
    sga_ref[...] = jax.nn.sigmoid(proj[:, O_GD:O_GA])
    sgb_ref[...] = jax.nn.sigmoid(proj[:, O_GA:])


def _inproj_call(x, mods, g, w_in, w2c, w0c, a2c, a0c, g2, *, tokens_per_batch, shared_row):
    n = x.shape[0]
    tm = min(SMALL_TILE, tokens_per_batch)
    w = RWKV_WIDTH

    def tok(width):
        return pl.BlockSpec((tm, width), lambda i: (i, 0))

    def tok2():
        return pl.BlockSpec((2, tm, w), lambda i: (0, i, 0))

    def shp(width):
        return jax.ShapeDtypeStruct((n, width), F32)

    shp2 = jax.ShapeDtypeStruct((2, n, w), F32)
    return pl.pallas_call(
        _inproj_kernel,
        grid=(n // tm,),
        in_specs=[tok(D_MODEL), _mod_spec(n // tm, tokens_per_batch // tm, shared_row), _resident((1, D_MODEL)),
                  _resident((D_MODEL, IN_COLS)), _resident((128, 2 * RWKV_WIDTH)), _resident((1, 2 * RWKV_WIDTH)),
                  _resident((128, 2 * RWKV_WIDTH)), _resident((1, 2 * RWKV_WIDTH)), _resident((128, RWKV_WIDTH))],
        out_specs=[tok(S5_WIDTH), tok(3 * w), tok2(), tok2(), tok(w), tok(D_MODEL), tok(D_MODEL)],
        out_shape=[shp(S5_WIDTH), shp(3 * w), shp2, shp2, shp(w), shp(D_MODEL), shp(D_MODEL)],
        compiler_params=_cparams("parallel"),
        name="inproj",
    )(x, mods, g, w_in, w2c, w0c, a2c, a0c, g2)


def _prep_kernel(up_ref, mid_ref, dn_ref, ic_ref, cw_ref, kk_w_ref, ka_ref, rk_ref, seg_ref,
                 r_ref, v_ref, kk_ref, k2_ref, bonus_ref, *, width, vertical, tiles_per_image):
    tm = mid_ref.shape[0]
    cw = cw_ref[...]
    mid = mid_ref[...]
    xpos = lax.broadcasted_iota(jnp.int32, (tm, 1), 0) % width
    not_first = xpos != 0
    not_last = xpos != width - 1

    def row_taps(base, dy):
        left = jnp.where(not_first, pltpu.roll(base, 1, 0), 0.0)
        right = jnp.where(not_last, pltpu.roll(base, tm - 1, 0), 0.0)
        return left * cw[3 * dy:3 * dy + 1] + base * cw[3 * dy + 1:3 * dy + 2] + right * cw[3 * dy + 2:3 * dy + 3]

    acc = row_taps(mid, 1)
    if vertical:
        t = pl.program_id(0) % tiles_per_image
        up = jnp.where(t == 0, 0.0, up_ref[...])
        dn = jnp.where(t == tiles_per_image - 1, 0.0, dn_ref[...])
        ext = jnp.concatenate([up, mid, dn], axis=0)
        acc = acc + row_taps(ext[0:tm], 0) + row_taps(ext[2 * width:2 * width + tm], 2)

    w = RWKV_WIDTH
    r, k, v = acc[:, :w], acc[:, w:2 * w], acc[:, 2 * w:]
    seg = seg_ref[...]
    kk = k * kk_w_ref[...]
    kk = kk * lax.rsqrt(_dot_right01(kk * kk, seg) + 1e-12)
    ka = ka_ref[...]
    rrk = r * rk_ref[...]
    kf = k * (1.0 + (ic_ref[0] - 1.0) * ka)
    kb = k * (1.0 + (ic_ref[1] - 1.0) * ka)
    r_ref[...] = r
    v_ref[...] = v
    kk_ref[...] = kk
    k2_ref[0] = kf
    k2_ref[1] = kb
    bonus_ref[...] = _dot_right01(rrk * kf + rrk * kb, seg) * v


def _prep_call(rkv, iclr, cw, kk_w, ka, rk, seg, *, tokens_per_batch, width, vertical):
    n = rkv.shape[0]
    tm = min(SMALL_TILE, tokens_per_batch)
    rows_per_tile = tm // width
    n_rows = n // width
    w3 = 3 * RWKV_WIDTH
    w = RWKV_WIDTH
    tok = pl.BlockSpec((tm, w), lambda i: (i, 0))
    return pl.pallas_call(
        functools.partial(_prep_kernel, width=width, vertical=vertical, tiles_per_image=tokens_per_batch // tm),
        grid=(n // tm,),
        in_specs=[pl.BlockSpec((width, w3), lambda i: (jnp.maximum(i * rows_per_tile - 1, 0), 0)),
                  pl.BlockSpec((tm, w3), lambda i: (i, 0)),
                  pl.BlockSpec((width, w3), lambda i: (jnp.minimum((i + 1) * rows_per_tile, n_rows - 1), 0)),
                  pl.BlockSpec((2, tm, w), lambda i: (0, i, 0)),
                  _resident((9, w3)), _resident((1, w)), _resident((1, w)), _resident((1, w)), _resident((w, w))],
        out_specs=[tok, tok, tok, pl.BlockSpec((2, tm, w), lambda i: (0, i, 0)), tok],
        out_shape=[jax.ShapeDtypeStruct((n, w), F32)] * 3 + [jax.ShapeDtypeStruct((2, n, w), F32),
                                                              jax.ShapeDtypeStruct((n, w), F32)],
        compiler_params=_cparams("parallel"),
        name="prep",
    )(rkv, rkv, rkv, iclr, cw, kk_w, ka, rk, seg)


def _s5_kernel(u_ref, wcat_ref, tfb_ref, cp_ref, ab_ref, y_ref, x_scr, h_scr, *, ctx_chunks, x_chunks, nb):
    u = u_ref[0]
    x_scr[...] = _dot(u, wcat_ref[0])
    ab = ab_ref[0]
    total = ctx_chunks + x_chunks

    def scan(carry, first, count, step, a1, a2, col):
        def body(i, hc):
            h, hs = hc
            off = pl.multiple_of((first + i * step) * nb, nb)
            h_scr[pl.ds(off, nb), col * 128:(col + 1) * 128] = h
            x = x_scr[pl.ds(off, nb), 2 * col * 128:(2 * col + 1) * 128]
            xs = x_scr[pl.ds(off, nb), (2 * col + 1) * 128:(2 * col + 2) * 128]
            return a1 * h + a2 * hs + x, a1 * hs - a2 * h + xs
        return lax.fori_loop(0, count, body, carry)

    zero = jnp.zeros((nb, 128), F32)
    scan((zero, zero), 0, total, 1, ab[0], ab[1], 0)
    carry = scan((zero, zero), ctx_chunks - 1, ctx_chunks, -1, ab[2], ab[3], 1)
    scan(carry, total - 1, x_chunks, -1, ab[2], ab[3], 1)

    r0 = ctx_chunks * nb
    y_ref[0] = _dot(u[r0:], tfb_ref[0]) + _dot(h_scr[r0:, :].astype(BF16), cp_ref[0])


def _s5_call(u_t, wcat, tfb, cp, ab, *, ctx_chunks, x_chunks, nb):
    g = u_t.shape[0]
    rows = (ctx_chunks + x_chunks) * nb
    lw = S5_CHUNK * S5_GROUP
    return pl.pallas_call(
        functools.partial(_s5_kernel, ctx_chunks=ctx_chunks, x_chunks=x_chunks, nb=nb),
        grid=(g,),
        in_specs=[pl.BlockSpec((1, rows, lw), lambda i: (i, 0, 0)),
                  pl.BlockSpec((1, lw, 512), lambda i: (i, 0, 0)),
                  pl.BlockSpec((1, lw, lw), lambda i: (i, 0, 0)),
                  pl.BlockSpec((1, 256, lw), lambda i: (i, 0, 0)),
                  pl.BlockSpec((1, 4, nb, 128), lambda i: (i, 0, 0, 0))],
        out_specs=pl.BlockSpec((1, x_chunks * nb, lw), lambda i: (i, 0, 0)),
        out_shape=jax.ShapeDtypeStruct((g, x_chunks * nb, lw), F32),
        scratch_shapes=[pltpu.VMEM((rows, 512), F32), pltpu.VMEM((rows, 256), F32)],
        compiler_params=_cparams("parallel"),
        name="s5",
    )(u_t, wcat, tfb, cp, ab)


def _s5_matrices(lam_re, lam_im, log_dt, b_re, b_im, c_re, c_im, nb):
    L, P, GS = S5_CHUNK, S5_STATE, S5_GROUP
    hp = lax.Precision.HIGHEST
    dt = jnp.exp(log_dt)[..., None]
    ab_re = jnp.exp(dt * lam_re) * jnp.cos(dt * lam_im)
    ab_im = jnp.exp(dt * lam_re) * jnp.sin(dt * lam_im)
    den = lam_re * lam_re + lam_im * lam_im
    z_re = ((ab_re - 1.0) * lam_re + ab_im * lam_im) / den
    z_im = (ab_im * lam_re - (ab_re - 1.0) * lam_im) / den
    bb_re = z_re[..., None] * b_re - z_im[..., None] * b_im
    bb_im = z_re[..., None] * b_im + z_im[..., None] * b_re
    k = jnp.arange(L + 1, dtype=F32)[:, None, None, None]
    pw_re = jnp.exp(k * dt * lam_re) * jnp.cos(k * dt * lam_im)
    pw_im = jnp.exp(k * dt * lam_re) * jnp.sin(k * dt * lam_im)
    ca_re = c_re[None] * pw_re[:, :, :, None, :] - c_im[None] * pw_im[:, :, :, None, :]
    ca_im = c_re[None] * pw_im[:, :, :, None, :] + c_im[None] * pw_re[:, :, :, None, :]
    ab_re_k = pw_re[..., None] * bb_re[None] - pw_im[..., None] * bb_im[None]
    ab_im_k = pw_re[..., None] * bb_im[None] + pw_im[..., None] * bb_re[None]
    kern = (jnp.einsum('kdgip,dgpj->dgkij', ca_re[:L], bb_re, precision=hp)
            - jnp.einsum('kdgip,dgpj->dgkij', ca_im[:L], bb_im, precision=hp))

    s = jnp.arange(L)[:, None]
    t = jnp.arange(L)[None, :]

    def toeplitz(kd, lag, valid):
        m = jnp.where(valid[None, :, :, None, None], kd[:, jnp.clip(lag, 0, L - 1)], 0.0)
        return m.transpose(0, 1, 4, 2, 3).reshape(-1, L * GS, L * GS)

    tfb = toeplitz(kern[0], t - s, t >= s) + toeplitz(kern[1], s - t, s >= t)

    def state_in(d, powers):
        wr = ab_re_k[powers, d].transpose(1, 0, 3, 2).reshape(-1, L * GS, P)
        wi = ab_im_k[powers, d].transpose(1, 0, 3, 2).reshape(-1, L * GS, P)
        return jnp.concatenate([wr, wi, wi, wr], axis=-1)

    wcat = jnp.concatenate([state_in(0, L - 1 - jnp.arange(L)), state_in(1, jnp.arange(L))], axis=-1)

    def state_out(d, powers):
        cr = ca_re[powers, d].transpose(1, 3, 0, 2).reshape(-1, P, L * GS)
        ci = ca_im[powers, d].transpose(1, 3, 0, 2).reshape(-1, P, L * GS)
        return jnp.concatenate([cr, -ci], axis=1)

    cp = jnp.concatenate([state_out(0, 1 + jnp.arange(L)), state_out(1, L - jnp.arange(L))], axis=1)

    def step_coeffs(d):
        a1 = jnp.concatenate([pw_re[L, d], pw_re[L, d]], axis=-1)
        a2 = jnp.concatenate([-pw_im[L, d], pw_im[L, d]], axis=-1)
        return [a1, a2]

    ab = jnp.stack(step_coeffs(0) + step_coeffs(1), axis=1)
    ab = jnp.broadcast_to(ab[:, :, None, :], ab.shape[:2] + (nb, 128))
    return wcat.astype(BF16), tfb.astype(BF16), cp.astype(BF16), ab


def _rwkv_kernel(rf_ref, vf_ref, kkf_ref, rb_ref, vb_ref, kkb_ref, kf_ref, kb_ref, lwf_ref, lwb_ref,
                 icf_ref, icb_ref, s0_ref, yf_ref, yb_ref, sT_ref, s_scr):
    c = pl.program_id(1)
    L, N, H = RWKV_CHUNK, RWKV_HEAD, RWKV_HEADS

    @pl.when(c == 0)
    def _():
        s_scr[...] = s0_ref[:, 0]

    row = lax.broadcasted_iota(jnp.int32, (L, L), 0)
    col = lax.broadcasted_iota(jnp.int32, (L, L), 1)
    eye = (col == row).astype(F32)
    strict = (col < row, col > row)
    incl = (col <= row, col >= row)

    chains = [(d, h) for d in range(2) for h in range(H)]
    a_t, r_t, v_f, b_t, k_t, b_h, k_h, w_tot = [], [], [], [], [], [], [], []
    per_dir = ((rf_ref, vf_ref, kkf_ref, kf_ref, lwf_ref, icf_ref), (rb_ref, vb_ref, kkb_ref, kb_ref, lwb_ref, icb_ref))
    for d, (r_ref, v_ref, kk_ref, k_ref, lw_ref, ic_ref) in enumerate(per_dir):
        lw = lw_ref[...]
        cum = _dot_left01(jnp.where(incl[d], 1.0, 0.0).astype(BF16), lw)
        tot = jnp.sum(lw, axis=0, keepdims=True)
        kk = kk_ref[...]
        b = kk * ic_ref[...]
        k = k_ref[...]
        e_ninc = jnp.exp(-cum)
        e_rem = jnp.exp(tot - cum)
        a_t.append(-kk * jnp.exp(cum - lw))
        r_t.append(r_ref[...] * jnp.exp(cum))
        v_f.append(v_ref[...])
        b_t.append((b * e_ninc).astype(BF16))
        k_t.append((k * e_ninc).astype(BF16))
        b_h.append((b * e_rem).astype(BF16))
        k_h.append((k * e_rem).astype(BF16))
        w_tot.append(jnp.exp(tot))

    def head(xs):
        return [xs[d][:, h * N:(h + 1) * N] for d, h in chains]

    a_th, r_th, vh, b_th, k_th, b_hh, k_hh, w_h = map(head, (a_t, r_t, v_f, b_t, k_t, b_h, k_h, w_tot))
    vhb = [x.astype(BF16) for x in vh]
    ar = [jnp.concatenate([a, r], axis=0).astype(BF16) for a, r in zip(a_th, r_th)]
    g1 = [_dot_nt(x, y) for x, y in zip(ar, b_th)]
    g2 = [_dot_nt(x, y) for x, y in zip(ar, k_th)]
    n_ab = [jnp.where(strict[d], g[:L], 0.0) for (d, _), g in zip(chains, g1)]
    a_ak = [jnp.where(strict[d], g[:L], 0.0).astype(BF16) for (d, _), g in zip(chains, g2)]
    a_rb = [jnp.where(incl[d], g[L:], 0.0).astype(BF16) for (d, _), g in zip(chains, g1)]
    a_rk = [jnp.where(incl[d], g[L:], 0.0).astype(BF16) for (d, _), g in zip(chains, g2)]
    akv = [_dot(x, y).astype(BF16) for x, y in zip(a_ak, vhb)]
    inv = [eye + x for x in n_ab]
    pw = [x.astype(BF16) for x in n_ab]
    for _ in range(5):
        pw = [_dot(x, x).astype(BF16) for x in pw]
        inv = [t + _dot(t.astype(BF16), p) for t, p in zip(inv, pw)]
    invb = [x.astype(BF16) for x in inv]
    a_p = [_dot(t, a.astype(BF16)) for t, a in zip(invb, a_th)]
    v_p = [_dot(t, x) for t, x in zip(invb, akv)]
    a_pb = [x.astype(BF16) for x in a_p]
    v_pb = [x.astype(BF16) for x in v_p]
    p_lr = [_dot(x.T.astype(BF16), y).astype(BF16) for x, y in zip(a_p, b_hh)]
    q = [_dot(x.T.astype(BF16), y) + _dot(z.T.astype(BF16), w)
         for x, y, z, w in zip(v_p, b_hh, vh, k_hh)]
    r_p = [(r + _dot(x, y)).astype(BF16) for r, x, y in zip(r_th, a_rb, a_pb)]
    y_loc = [_dot(x, y) + _dot(z, w) for x, y, z, w in zip(a_rb, v_pb, a_rk, vhb)]

    s = [s_scr[d, h] for d, h in chains]
    sb = [x.astype(BF16) for x in s]
    y = [_dot_nt(x, z) + yl for x, z, yl in zip(r_p, sb, y_loc)]
    s_new = [x * w + _dot(z, p) + qq for x, w, z, p, qq in zip(s, w_h, sb, p_lr, q)]
    for (d, h), yy, ss in zip(chains, y, s_new):
        (yf_ref, yb_ref)[d][:, h * N:(h + 1) * N] = yy
        s_scr[d, h] = ss

    @pl.when(c == pl.num_programs(1) - 1)
    def _():
        sT_ref[:, 0] = s_scr[...]


def _rwkv_call(r, v, kk, k2, lw2, ic2, s0, *, tokens_per_batch):
    n = r.shape[0]
    nb = n // tokens_per_batch
    nc = tokens_per_batch // RWKV_CHUNK
    w = RWKV_WIDTH

    fwd = pl.BlockSpec((RWKV_CHUNK, w), lambda b, c: (b * nc + c, 0))
    bwd = pl.BlockSpec((RWKV_CHUNK, w), lambda b, c: (b * nc + nc - 1 - c, 0))

    both = [pl.BlockSpec((None, RWKV_CHUNK, w), lambda b, c: (0, b * nc + c, 0)),
            pl.BlockSpec((None, RWKV_CHUNK, w), lambda b, c: (1, b * nc + nc - 1 - c, 0))]
    st = pl.BlockSpec((2, 1, RWKV_HEADS, RWKV_HEAD, RWKV_HEAD), lambda b, c: (0, b, 0, 0, 0))
    return pl.pallas_call(
        _rwkv_kernel,
        grid=(nb, nc),
        in_specs=[fwd, fwd, fwd, bwd, bwd, bwd, *both, *both, *both, st],
        out_specs=[fwd, bwd, st],
        out_shape=[jax.ShapeDtypeStruct((n, w), F32), jax.ShapeDtypeStruct((n, w), F32),
                   jax.ShapeDtypeStruct((2, nb, RWKV_HEADS, RWKV_HEAD, RWKV_HEAD), F32)],
        scratch_shapes=[pltpu.VMEM((2, RWKV_HEADS, RWKV_HEAD, RWKV_HEAD), F32)],
        compiler_params=_cparams("parallel", "arbitrary"),
        name="rwkv",
    )(r, v, kk, r, v, kk, k2, k2, lw2, lw2, ic2, ic2, s0)


def _out_kernel(x_ref, mod_ref, ya_ref, u_ref, ybf_ref, ybb_ref, bonus_ref, gg_ref, sga_ref, sgb_ref,
                s5d_ref, wglu_ref, wproj_ref, lng_ref, lnb_ref, seg_ref, wo_ref, wout_ref, o_ref):
    m = mod_ref[0]
    ya = jax.nn.gelu(ya_ref[...] + s5d_ref[...] * u_ref[...])
    ya = ya * jax.nn.sigmoid(_dot(ya.astype(BF16), wglu_ref[...]))
    pa = _dot(ya.astype(BF16), wproj_ref[...])

    seg = seg_ref[...]
    yb = ybf_ref[...] + ybb_ref[...]
    mu = _dot_right01(yb, seg) * (1.0 / RWKV_HEAD)
    cen = yb - mu
    var = _dot_right01(cen * cen, seg) * (1.0 / RWKV_HEAD)
    yb = cen * lax.rsqrt(var + LN_X_EPS) * lng_ref[...] + lnb_ref[...] + bonus_ref[...]
    pb = _dot((yb * gg_ref[...]).astype(BF16), wo_ref[...])

    merged = sga_ref[...] * pa + sgb_ref[...] * pb
    o_ref[...] = x_ref[...] + m[5:6] * _dot(merged.astype(BF16), wout_ref[...])


def _out_call(x, mods, ya, u, ybf, ybb, bonus, gg, sga, sgb, s5d, wglu, wproj, lng, lnb, seg, wo, wout,
              *, tokens_per_batch):
    n = x.shape[0]
    tm = min(SMALL_TILE, tokens_per_batch)
    w = RWKV_WIDTH

    def tok(width):
        return pl.BlockSpec((tm, width), lambda i: (i, 0))

    return pl.pallas_call(
        _out_kernel,
        grid=(n // tm,),
        in_specs=[tok(D_MODEL), _mod_spec(n // tm, tokens_per_batch // tm, None), tok(w), tok(w),
                  tok(w), tok(w), tok(w), tok(w), tok(D_MODEL), tok(D_MODEL),
                  _resident((1, w)), _resident((w, w)), _resident((w, D_MODEL)), _resident((1, w)),
                  _resident((1, w)), _resident((w, w)), _resident((w, D_MODEL)), _resident((D_MODEL, D_MODEL))],
        out_specs=tok(D_MODEL),
        out_shape=jax.ShapeDtypeStruct((n, D_MODEL), F32),
        compiler_params=_cparams("parallel"),
        name="outk",
    )(x, mods, ya, u, ybf, ybb, bonus, gg, sga, sgb, s5d, wglu, wproj, lng, lnb, seg, wo, wout)


def _block_diag2(a):
    z = jnp.zeros_like(a[0])
    return jnp.concatenate([jnp.concatenate([a[0], z], axis=1), jnp.concatenate([z, a[1]], axis=1)], axis=0)


def _to_s5_layout(u, nb):
    n = u.shape[0] // nb
    x = u.reshape(nb, n // S5_CHUNK, S5_CHUNK, S5_GROUPS, S5_GROUP).transpose(3, 1, 0, 2, 4)
    return x.reshape(S5_GROUPS, (n // S5_CHUNK) * nb, S5_CHUNK * S5_GROUP).astype(BF16)


def _from_s5_layout(y, nb):
    nc = y.shape[1] // nb
    x = y.reshape(S5_GROUPS, nc, nb, S5_CHUNK, S5_GROUP).transpose(2, 1, 3, 0, 4)
    return x.reshape(nb * nc * S5_CHUNK, S5_WIDTH)


def kernel(x, c, ctx, c_ctx, w_mod, b_mod, norm_g, ffn_w_gate, ffn_w_up, ffn_w_down, w_in, s5_A_re, s5_A_im, s5_log_dt, s5_B_re, s5_B_im, s5_C_re, s5_C_im, s5_D, s5_w_glu, s5_w_proj, rwkv_conv, rwkv_w0, rwkv_w2, rwkv_a0, rwkv_a2, rwkv_g2, rwkv_k_k, rwkv_k_a, rwkv_r_k, rwkv_ln_g, rwkv_ln_b, rwkv_w_o, w_out, final_g):
    nb, seq, d = x.shape
    n_ctx = ctx.shape[1]
    l = 0
    w = RWKV_WIDTH

    mod_rows = 16
    cc = jnp.concatenate([c, c_ctx[None], jnp.zeros((mod_rows - nb - 1, d), F32)], axis=0)
    mods = _mod_call(cc, w_mod[l], b_mod[l][None]).reshape(mod_rows, N_MOD, d)

    xt = x.reshape(nb * seq, d)
    ct = ctx.reshape(nb * n_ctx, d)
    row = lambda a: a.reshape(1, -1)
    bf = lambda a: a.astype(BF16)

    f1 = (row(norm_g[l, 0]), bf(ffn_w_gate[l, 0]), bf(ffn_w_up[l, 0]), bf(ffn_w_down[l, 0]), row(final_g))
    x1 = _ffn_call(xt, mods, *f1, j=0, tokens_per_batch=seq, shared_row=None, final_norm=False)
    c1 = _ffn_call(ct, mods, *f1, j=0, tokens_per_batch=n_ctx, shared_row=nb, final_norm=False)

    ip = (row(norm_g[l, 1]), bf(w_in[l]), bf(_block_diag2(rwkv_w2[l])), row(rwkv_w0[l]),
          bf(_block_diag2(rwkv_a2[l])), row(rwkv_a0[l]), bf(rwkv_g2[l]))
    u_x, rkv_x, lw_x, ic_x, gg_x, sga_x, sgb_x = _inproj_call(x1, mods, *ip, tokens_per_batch=seq, shared_row=None)
    u_c, rkv_c, lw_c, ic_c, _, _, _ = _inproj_call(c1, mods, *ip, tokens_per_batch=n_ctx, shared_row=nb)

    s5p = _s5_matrices(s5_A_re[l], s5_A_im[l], s5_log_dt[l], s5_B_re[l], s5_B_im[l], s5_C_re[l], s5_C_im[l], nb)
    u_t = jnp.concatenate([_to_s5_layout(u_c, nb), _to_s5_layout(u_x, nb)], axis=1)
    ya = _from_s5_layout(_s5_call(u_t, *s5p, ctx_chunks=n_ctx // S5_CHUNK, x_chunks=seq // S5_CHUNK, nb=nb), nb)

    seg = (jnp.arange(w)[:, None] // RWKV_HEAD == jnp.arange(w)[None, :] // RWKV_HEAD).astype(BF16)
    pp = (rwkv_conv[l].reshape(9, 3 * w), row(rwkv_k_k[l]), row(rwkv_k_a[l]), row(rwkv_r_k[l]), seg)
    r_c, v_c, kk_c, k2_c, _ = _prep_call(rkv_c, ic_c, *pp, tokens_per_batch=n_ctx, width=n_ctx, vertical=False)
    r_x, v_x, kk_x, k2_x, bonus = _prep_call(rkv_x, ic_x, *pp, tokens_per_batch=seq, width=GRID_W, vertical=True)

    s0 = jnp.zeros((2, nb, RWKV_HEADS, RWKV_HEAD, RWKV_HEAD), F32)
    _, _, s_ctx = _rwkv_call(r_c, v_c, kk_c, k2_c, lw_c, ic_c, s0, tokens_per_batch=n_ctx)
    ybf, ybb, _ = _rwkv_call(r_x, v_x, kk_x, k2_x, lw_x, ic_x, s_ctx, tokens_per_batch=seq)

    x2 = _out_call(x1, mods, ya, u_x, ybf, ybb, bonus, gg_x, sga_x, sgb_x, row(s5_D[l]), bf(s5_w_glu[l]),
                   bf(s5_w_proj[l]), row(rwkv_ln_g[l]), row(rwkv_ln_b[l]), seg, bf(rwkv_w_o[l]), bf(w_out[l]),
                   tokens_per_batch=seq)

    f2 = (row(norm_g[l, 2]), bf(ffn_w_gate[l, 1]), bf(ffn_w_up[l, 1]), bf(ffn_w_down[l, 1]), row(final_g))
    out = _ffn_call(x2, mods, *f2, j=2, tokens_per_batch=seq, shared_row=None, final_norm=True)
    return out.reshape(nb, seq, d)
```

```python
import functools
import math

import jax
import jax.numpy as jnp
from jax import lax
from jax.experimental import pallas as pl
from jax.experimental.pallas import tpu as pltpu

F32 = jnp.float32
BF16 = jnp.bfloat16

D_MODEL = 1024
N_MOD = 9
FFN_DIM = 2816
RMS_EPS = 1e-6
GRID_W = 64
S5_WIDTH = 512
S5_GROUP = 16
S5_GROUPS = 32
S5_STATE = 64
RWKV_WIDTH = 512
RWKV_HEAD = 64
RWKV_HEADS = 8
LN_X_EPS = 64e-5
LANE = 128

S5_CHUNK = 16
RWKV_CHUNK = 64
TOKEN_TILE = 512
SMALL_TILE = 256
BATCH_TILE = 32
VMEM_LIMIT = 56 * 1024 * 1024


def _cparams(*sem):
    return pltpu.CompilerParams(dimension_semantics=sem, vmem_limit_bytes=VMEM_LIMIT)


def _resident(shape):
    nd = len(shape)
    return pl.BlockSpec(shape, lambda *_: (0,) * nd, pipeline_mode=pl.Buffered(1))


def _dot(a, b):
    return jnp.dot(a, b, preferred_element_type=F32)


def _dot_nt(a, b):
    return lax.dot_general(a, b, (((1,), (1,)), ((), ())), preferred_element_type=F32)


def _split3(x):
    hi = x.astype(BF16)
    r1 = x - hi.astype(F32)
    mid = r1.astype(BF16)
    lo = (r1 - mid.astype(F32)).astype(BF16)
    return hi, mid, lo


def _dot_left01(m01, x):
    hi, mid, lo = _split3(x)
    return _dot(m01, hi) + _dot(m01, mid) + _dot(m01, lo)


def _dot_right01(x, m01):
    hi, mid, lo = _split3(x)
    return _dot(hi, m01) + _dot(mid, m01) + _dot(lo, m01)


def _rms_mod(x, g, shift, scale):
    y = x * lax.rsqrt(jnp.mean(x * x, axis=-1, keepdims=True) + RMS_EPS)
    return (y * g) * (1.0 + scale) + shift


def _mod_kernel(c_ref, w_ref, b_ref, o_ref):
    c = c_ref[...]
    s = c * jax.nn.sigmoid(c)
    o_ref[...] = jnp.dot(s, w_ref[...], precision=lax.Precision.HIGHEST, preferred_element_type=F32) + b_ref[...]


def _mod_call(cc, w_mod, b_mod):
    rows = cc.shape[0]
    return pl.pallas_call(
        _mod_kernel,
        grid=(N_MOD,),
        in_specs=[pl.BlockSpec((rows, D_MODEL), lambda j: (0, 0)),
                  pl.BlockSpec((D_MODEL, D_MODEL), lambda j: (0, j)),
                  pl.BlockSpec((1, D_MODEL), lambda j: (0, j))],
        out_specs=pl.BlockSpec((rows, D_MODEL), lambda j: (0, j)),
        out_shape=jax.ShapeDtypeStruct((rows, N_MOD * D_MODEL), F32),
        compiler_params=_cparams("arbitrary"),
        name="mod",
    )(cc, w_mod, b_mod)


def _ffn_kernel(x_ref, mod_ref, g_ref, wg_ref, wu_ref, wd_ref, fg_ref, o_ref, *, j, final_norm):
    x = x_ref[...]
    m = mod_ref[0]
    h = _rms_mod(x, g_ref[...], m[3 * j:3 * j + 1], m[3 * j + 1:3 * j + 2]).astype(BF16)
    a = _dot(h, wg_ref[...])
    a = (a * jax.nn.sigmoid(a)) * _dot(h, wu_ref[...])
    y = _dot(a.astype(BF16), wd_ref[...])
    out = x + 0.5 * m[3 * j + 2:3 * j + 3] * y
    if final_norm:
        out = out * lax.rsqrt(jnp.mean(out * out, axis=-1, keepdims=True) + RMS_EPS) * fg_ref[...]
    o_ref[...] = out


def _mod_spec(n_tiles, tiles_per_batch, shared_row):
    if shared_row is not None:
        return pl.BlockSpec((1, N_MOD, D_MODEL), lambda i: (shared_row, 0, 0))
    assert n_tiles % tiles_per_batch == 0
    return pl.BlockSpec((1, N_MOD, D_MODEL), lambda i: (i // tiles_per_batch, 0, 0))


def _ffn_call(x, mods, g, wg, wu, wd, fg, *, j, tokens_per_batch, shared_row, final_norm):
    n = x.shape[0]
    tm = min(TOKEN_TILE, tokens_per_batch)
    tok = pl.BlockSpec((tm, D_MODEL), lambda i: (i, 0))
    return pl.pallas_call(
        functools.partial(_ffn_kernel, j=j, final_norm=final_norm),
        grid=(n // tm,),
        in_specs=[tok, _mod_spec(n // tm, tokens_per_batch // tm, shared_row), _resident((1, D_MODEL)),
                  _resident((D_MODEL, FFN_DIM)), _resident((D_MODEL, FFN_DIM)), _resident((FFN_DIM, D_MODEL)),
                  _resident((1, D_MODEL))],
        out_specs=tok,
        out_shape=jax.ShapeDtypeStruct((n, D_MODEL), F32),
        compiler_params=_cparams("parallel"),
        name="ffn",
    )(x, mods, g, wg, wu, wd, fg)


O_U = S5_WIDTH
O_RKV = O_U + 3 * RWKV_WIDTH
O_WD = O_RKV + 128
O_AD = O_WD + 128
O_GD = O_AD + 128
O_GA = O_GD + D_MODEL
IN_COLS = O_GA + D_MODEL
DECAY_SCALE = math.exp(-0.5)


def _inproj_kernel(x_ref, mod_ref, g_ref, w_ref, w2_ref, w0_ref, a2_ref, a0_ref, g2_ref,
                   u_ref, us5_ref, rkv_ref, lw_ref, ic_ref, gg_ref, sga_ref, sgb_ref, u_scr):
    nb, tn, _ = x_ref.shape
    rows = nb * tn
    m = mod_ref[...]
    h = _rms_mod(x_ref[...], g_ref[...], m[:, 3:4], m[:, 4:5]).reshape(rows, D_MODEL).astype(BF16)
    proj = _dot(h, w_ref[...])

    def put(ref, val):
        ref[...] = val.reshape(nb, tn, val.shape[-1]).astype(ref.dtype)

    u = proj[:, :O_U]
    put(u_ref, u)
    put(rkv_ref, proj[:, O_U:O_RKV])
    w_log = w0_ref[...] + _dot(jnp.tanh(proj[:, O_RKV:O_WD]).astype(BF16), w2_ref[...])
    lw = -DECAY_SCALE * jax.nn.sigmoid(w_log)
    ic = jax.nn.sigmoid(a0_ref[...] + _dot(proj[:, O_WD:O_AD].astype(BF16), a2_ref[...]))
    for d in range(2):
        lw_ref[d] = lw[:, d * RWKV_WIDTH:(d + 1) * RWKV_WIDTH].reshape(nb, tn, RWKV_WIDTH)
        ic_ref[d] = ic[:, d * RWKV_WIDTH:(d + 1) * RWKV_WIDTH].reshape(nb, tn, RWKV_WIDTH)
    put(gg_ref, _dot(jax.nn.sigmoid(proj[:, O_AD:O_GD]).astype(BF16), g2_ref[...]))
    put(sga_ref, jax.nn.sigmoid(proj[:, O_GD:O_GA]))
    put(sgb_ref, jax.nn.sigmoid(proj[:, O_GA:]))

    per_tile = LANE // S5_GROUP
    for q in range(S5_WIDTH // LANE):
        u_scr[q] = u[:, q * LANE:(q + 1) * LANE]
    for q in range(S5_WIDTH // LANE):
        taps = [[u_scr[q, pl.ds(cl * S5_CHUNK + t, nb, stride=tn), :] for t in range(S5_CHUNK)]
                for cl in range(tn // S5_CHUNK)]
        for gq in range(per_tile):
            sl = slice(gq * S5_GROUP, (gq + 1) * S5_GROUP)
            rows_g = [jnp.concatenate([a[:, sl] for a in chunk_taps], axis=1) for chunk_taps in taps]
            us5_ref[q * per_tile + gq] = jnp.concatenate(rows_g, axis=0).astype(BF16)


def _inproj_call(x, mods, g, w_in, w2c, w0c, a2c, a0c, g2):
    nb, n, _ = x.shape
    tn = BATCH_TILE
    w = RWKV_WIDTH
    lw5 = S5_CHUNK * S5_GROUP

    def tok(width):
        return pl.BlockSpec((nb, tn, width), lambda i: (0, i, 0))

    tok2 = pl.BlockSpec((2, nb, tn, w), lambda i: (0, 0, i, 0))

    def shp(width, dtype=F32):
        return jax.ShapeDtypeStruct((nb, n, width), dtype)

    shp2 = jax.ShapeDtypeStruct((2, nb, n, w), F32)
    s5_rows = tn // S5_CHUNK * nb
    return pl.pallas_call(
        _inproj_kernel,
        grid=(n // tn,),
        in_specs=[tok(D_MODEL), _resident((nb, N_MOD, D_MODEL)), _resident((1, D_MODEL)),
                  _resident((D_MODEL, IN_COLS)), _resident((128, 2 * RWKV_WIDTH)), _resident((1, 2 * RWKV_WIDTH)),
                  _resident((128, 2 * RWKV_WIDTH)), _resident((1, 2 * RWKV_WIDTH)), _resident((128, RWKV_WIDTH))],
        out_specs=[tok(S5_WIDTH), pl.BlockSpec((S5_GROUPS, s5_rows, lw5), lambda i: (0, i, 0)), tok(3 * w),
                   tok2, tok2, tok(w), tok(D_MODEL), tok(D_MODEL)],
        out_shape=[shp(S5_WIDTH), jax.ShapeDtypeStruct((S5_GROUPS, n // S5_CHUNK * nb, lw5), BF16), shp(3 * w),
                   shp2, shp2, shp(w), shp(D_MODEL, BF16), shp(D_MODEL, BF16)],
        scratch_shapes=[pltpu.VMEM((S5_WIDTH // LANE, nb * tn, LANE), F32)],
        compiler_params=_cparams("parallel"),
        name="inproj",
    )(x, mods, g, w_in, w2c, w0c, a2c, a0c, g2)


def _prep_kernel(up_ref, mid_ref, dn_ref, ic_ref, cw_ref, kk_w_ref, ka_ref, rk_ref, seg_ref,
                 r_ref, v_ref, kk_ref, k2_ref, bonus_ref, *, width, vertical, tiles_per_image):
    tm = mid_ref.shape[0]
    cw = cw_ref[...]
    mid = mid_ref[...]
    xpos = lax.broadcasted_iota(jnp.int32, (tm, 1), 0) % width
    not_first = xpos != 0
    not_last = xpos != width - 1

    def row_taps(base, dy):
        left = jnp.where(not_first, pltpu.roll(base, 1, 0), 0.0)
        right = jnp.where(not_last, pltpu.roll(base, tm - 1, 0), 0.0)
        return left * cw[3 * dy:3 * dy + 1] + base * cw[3 * dy + 1:3 * dy + 2] + right * cw[3 * dy + 2:3 * dy + 3]

    acc = row_taps(mid, 1)
    if vertical:
        t = pl.program_id(0) % tiles_per_image
        up = jnp.where(t == 0, 0.0, up_ref[...])
        dn = jnp.where(t == tiles_per_image - 1, 0.0, dn_ref[...])
        ext = jnp.concatenate([up, mid, dn], axis=0)
        acc = acc + row_taps(ext[0:tm], 0) + row_taps(ext[2 * width:2 * width + tm], 2)

    w = RWKV_WIDTH
    r, k, v = acc[:, :w], acc[:, w:2 * w], acc[:, 2 * w:]
    seg = seg_ref[...]
    kk = k * kk_w_ref[...]
    kk = kk * lax.rsqrt(_dot_right01(kk * kk, seg) + 1e-12)
    ka = ka_ref[...]
    rrk = r * rk_ref[...]
    kf = k * (1.0 + (ic_ref[0] - 1.0) * ka)
    kb = k * (1.0 + (ic_ref[1] - 1.0) * ka)
    r_ref[...] = r
    v_ref[...] = v
    kk_ref[...] = kk
    k2_ref[0] = kf
    k2_ref[1] = kb
    bonus_ref[...] = _dot_right01(rrk * kf + rrk * kb, seg) * v


def _prep_call(rkv, iclr, cw, kk_w, ka, rk, seg, *, tokens_per_batch, width, vertical):
    n = rkv.shape[0]
    tm = min(SMALL_TILE, tokens_per_batch)
    rows_per_tile = tm // width
    n_rows = n // width
    w3 = 3 * RWKV_WIDTH
    w = RWKV_WIDTH
    tok = pl.BlockSpec((tm, w), lambda i: (i, 0))
    return pl.pallas_call(
        functools.partial(_prep_kernel, width=width, vertical=vertical, tiles_per_image=tokens_per_batch // tm),
        grid=(n // tm,),
        in_specs=[pl.BlockSpec((width, w3), lambda i: (jnp.maximum(i * rows_per_tile - 1, 0), 0)),
                  pl.BlockSpec((tm, w3), lambda i: (i, 0)),
                  pl.BlockSpec((width, w3), lambda i: (jnp.minimum((i + 1) * rows_per_tile, n_rows - 1), 0)),
                  pl.BlockSpec((2, tm, w), lambda i: (0, i, 0)),
                  _resident((9, w3)), _resident((1, w)), _resident((1, w)), _resident((1, w)), _resident((w, w))],
        out_specs=[tok, tok, tok, pl.BlockSpec((2, tm, w), lambda i: (0, i, 0)), tok],
        out_shape=[jax.ShapeDtypeStruct((n, w), F32)] * 3 + [jax.ShapeDtypeStruct((2, n, w), F32),
                                                              jax.ShapeDtypeStruct((n, w), F32)],
        compiler_params=_cparams("parallel"),
        name="prep",
    )(rkv, rkv, rkv, iclr, cw, kk_w, ka, rk, seg)


def _s5_kernel(uc_ref, ux_ref, wcat_ref, tfb_ref, cp_ref, ab_ref, y_ref, x_scr, h_scr, *, ctx_chunks, x_chunks, nb):
    ux = ux_ref[0]
    r0 = ctx_chunks * nb
    x_scr[:r0] = _dot(uc_ref[0], wcat_ref[0])
    x_scr[r0:] = _dot(ux, wcat_ref[0])
    ab = ab_ref[0]
    total = ctx_chunks + x_chunks

    def scan(carry, first, count, step, a1, a2, col):
        def body(i, hc):
            h, hs = hc
            off = pl.multiple_of((first + i * step) * nb, nb)
            h_scr[pl.ds(off, nb), col * 128:(col + 1) * 128] = h
            x = x_scr[pl.ds(off, nb), 2 * col * 128:(2 * col + 1) * 128]
            xs = x_scr[pl.ds(off, nb), (2 * col + 1) * 128:(2 * col + 2) * 128]
            return a1 * h + a2 * hs + x, a1 * hs - a2 * h + xs
        return lax.fori_loop(0, count, body, carry)

    zero = jnp.zeros((nb, 128), F32)
    scan((zero, zero), 0, total, 1, ab[0], ab[1], 0)
    carry = scan((zero, zero), ctx_chunks - 1, ctx_chunks, -1, ab[2], ab[3], 1)
    scan(carry, total - 1, x_chunks, -1, ab[2], ab[3], 1)

    y_ref[0] = _dot(ux, tfb_ref[0]) + _dot(h_scr[r0:, :].astype(BF16), cp_ref[0])


def _s5_call(u_c, u_x, wcat, tfb, cp, ab, *, nb):
    g = u_x.shape[0]
    ctx_chunks = u_c.shape[1] // nb
    x_chunks = u_x.shape[1] // nb
    rows = (ctx_chunks + x_chunks) * nb
    lw = S5_CHUNK * S5_GROUP
    return pl.pallas_call(
        functools.partial(_s5_kernel, ctx_chunks=ctx_chunks, x_chunks=x_chunks, nb=nb),
        grid=(g,),
        in_specs=[pl.BlockSpec((1, ctx_chunks * nb, lw), lambda i: (i, 0, 0)),
                  pl.BlockSpec((1, x_chunks * nb, lw), lambda i: (i, 0, 0)),
                  pl.BlockSpec((1, lw, 512), lambda i: (i, 0, 0)),
                  pl.BlockSpec((1, lw, lw), lambda i: (i, 0, 0)),
                  pl.BlockSpec((1, 256, lw), lambda i: (i, 0, 0)),
                  pl.BlockSpec((1, 4, nb, 128), lambda i: (i, 0, 0, 0))],
        out_specs=pl.BlockSpec((1, x_chunks * nb, lw), lambda i: (i, 0, 0)),
        out_shape=jax.ShapeDtypeStruct((g, x_chunks * nb, lw), F32),
        scratch_shapes=[pltpu.VMEM((rows, 512), F32), pltpu.VMEM((rows, 256), F32)],
        compiler_params=_cparams("parallel"),
        name="s5",
    )(u_c, u_x, wcat, tfb, cp, ab)


def _s5_matrices(lam_re, lam_im, log_dt, b_re, b_im, c_re, c_im, nb):
    L, P, GS = S5_CHUNK, S5_STATE, S5_GROUP
    hp = lax.Precision.HIGHEST
    dt = jnp.exp(log_dt)[..., None]
    ab_re = jnp.exp(dt * lam_re) * jnp.cos(dt * lam_im)
    ab_im = jnp.exp(dt * lam_re) * jnp.sin(dt * lam_im)
    den = lam_re * lam_re + lam_im * lam_im
    z_re = ((ab_re - 1.0) * lam_re + ab_im * lam_im) / den
    z_im = (ab_im * lam_re - (ab_re - 1.0) * lam_im) / den
    bb_re = z_re[..., None] * b_re - z_im[..., None] * b_im
    bb_im = z_re[..., None] * b_im + z_im[..., None] * b_re
    k = jnp.arange(L + 1, dtype=F32)[:, None, None, None]
    pw_re = jnp.exp(k * dt * lam_re) * jnp.cos(k * dt * lam_im)
    pw_im = jnp.exp(k * dt * lam_re) * jnp.sin(k * dt * lam_im)
    ca_re = c_re[None] * pw_re[:, :, :, None, :] - c_im[None] * pw_im[:, :, :, None, :]
    ca_im = c_re[None] * pw_im[:, :, :, None, :] + c_im[None] * pw_re[:, :, :, None, :]
    ab_re_k = pw_re[..., None] * bb_re[None] - pw_im[..., None] * bb_im[None]
    ab_im_k = pw_re[..., None] * bb_im[None] + pw_im[..., None] * bb_re[None]
    kern = (jnp.einsum('kdgip,dgpj->dgkij', ca_re[:L], bb_re, precision=hp)
            - jnp.einsum('kdgip,dgpj->dgkij', ca_im[:L], bb_im, precision=hp))

    s = jnp.arange(L)[:, None]
    t = jnp.arange(L)[None, :]

    def toeplitz(kd, lag, valid):
        m = jnp.where(valid[None, :, :, None, None], kd[:, jnp.clip(lag, 0, L - 1)], 0.0)
        return m.transpose(0, 1, 4, 2, 3).reshape(-1, L * GS, L * GS)

    tfb = toeplitz(kern[0], t - s, t >= s) + toeplitz(kern[1], s - t, s >= t)

    def state_in(d, powers):
        wr = ab_re_k[powers, d].transpose(1, 0, 3, 2).reshape(-1, L * GS, P)
        wi = ab_im_k[powers, d].transpose(1, 0, 3, 2).reshape(-1, L * GS, P)
        return jnp.concatenate([wr, wi, wi, wr], axis=-1)

    wcat = jnp.concatenate([state_in(0, L - 1 - jnp.arange(L)), state_in(1, jnp.arange(L))], axis=-1)

    def state_out(d, powers):
        cr = ca_re[powers, d].transpose(1, 3, 0, 2).reshape(-1, P, L * GS)
        ci = ca_im[powers, d].transpose(1, 3, 0, 2).reshape(-1, P, L * GS)
        return jnp.concatenate([cr, -ci], axis=1)

    cp = jnp.concatenate([state_out(0, 1 + jnp.arange(L)), state_out(1, L - jnp.arange(L))], axis=1)

    def step_coeffs(d):
        a1 = jnp.concatenate([pw_re[L, d], pw_re[L, d]], axis=-1)
        a2 = jnp.concatenate([-pw_im[L, d], pw_im[L, d]], axis=-1)
        return [a1, a2]

    ab = jnp.stack(step_coeffs(0) + step_coeffs(1), axis=1)
    ab = jnp.broadcast_to(ab[:, :, None, :], ab.shape[:2] + (nb, 128))
    return wcat.astype(BF16), tfb.astype(BF16), cp.astype(BF16), ab


def _rwkv_kernel(rf_ref, vf_ref, kkf_ref, rb_ref, vb_ref, kkb_ref, kf_ref, kb_ref, lwf_ref, lwb_ref,
                 icf_ref, icb_ref, s0_ref, yf_ref, yb_ref, sT_ref, s_scr):
    c = pl.program_id(1)
    L, N, H = RWKV_CHUNK, RWKV_HEAD, RWKV_HEADS

    @pl.when(c == 0)
    def _():
        s_scr[...] = s0_ref[:, 0]

    row = lax.broadcasted_iota(jnp.int32, (L, L), 0)
    col = lax.broadcasted_iota(jnp.int32, (L, L), 1)
    eye = (col == row).astype(F32)
    strict = (col < row, col > row)
    incl = (col <= row, col >= row)

    chains = [(d, h) for d in range(2) for h in range(H)]
    a_t, r_t, v_f, b_t, k_t, b_h, k_h, w_tot = [], [], [], [], [], [], [], []
    per_dir = ((rf_ref, vf_ref, kkf_ref, kf_ref, lwf_ref, icf_ref), (rb_ref, vb_ref, kkb_ref, kb_ref, lwb_ref, icb_ref))
    for d, (r_ref, v_ref, kk_ref, k_ref, lw_ref, ic_ref) in enumerate(per_dir):
        lw = lw_ref[...]
        cum = _dot_left01(jnp.where(incl[d], 1.0, 0.0).astype(BF16), lw)
        tot = jnp.sum(lw, axis=0, keepdims=True)
        kk = kk_ref[...]
        b = kk * ic_ref[...]
        k = k_ref[...]
        e_ninc = jnp.exp(-cum)
        e_rem = jnp.exp(tot - cum)
        a_t.append(-kk * jnp.exp(cum - lw))
        r_t.append(r_ref[...] * jnp.exp(cum))
        v_f.append(v_ref[...])
        b_t.append((b * e_ninc).astype(BF16))
        k_t.append((k * e_ninc).astype(BF16))
        b_h.append((b * e_rem).astype(BF16))
        k_h.append((k * e_rem).astype(BF16))
        w_tot.append(jnp.exp(tot))

    def head(xs):
        return [xs[d][:, h * N:(h + 1) * N] for d, h in chains]

    a_th, r_th, vh, b_th, k_th, b_hh, k_hh, w_h = map(head, (a_t, r_t, v_f, b_t, k_t, b_h, k_h, w_tot))
    vhb = [x.astype(BF16) for x in vh]
    ar = [jnp.concatenate([a, r], axis=0).astype(BF16) for a, r in zip(a_th, r_th)]
    g1 = [_dot_nt(x, y) for x, y in zip(ar, b_th)]
    g2 = [_dot_nt(x, y) for x, y in zip(ar, k_th)]
    n_ab = [jnp.where(strict[d], g[:L], 0.0) for (d, _), g in zip(chains, g1)]
    a_ak = [jnp.where(strict[d], g[:L], 0.0).astype(BF16) for (d, _), g in zip(chains, g2)]
    a_rb = [jnp.where(incl[d], g[L:], 0.0).astype(BF16) for (d, _), g in zip(chains, g1)]
    a_rk = [jnp.where(incl[d], g[L:], 0.0).astype(BF16) for (d, _), g in zip(chains, g2)]
    akv = [_dot(x, y).astype(BF16) for x, y in zip(a_ak, vhb)]
    inv = [eye + x for x in n_ab]
    pw = [x.astype(BF16) for x in n_ab]
    for _ in range(5):
        pw = [_dot(x, x).astype(BF16) for x in pw]
        inv = [t + _dot(t.astype(BF16), p) for t, p in zip(inv, pw)]
    invb = [x.astype(BF16) for x in inv]
    a_p = [_dot(t, a.astype(BF16)) for t, a in zip(invb, a_th)]
    v_p = [_dot(t, x) for t, x in zip(invb, akv)]
    a_pb = [x.astype(BF16) for x in a_p]
    v_pb = [x.astype(BF16) for x in v_p]
    p_lr = [_dot(x.T.astype(BF16), y).astype(BF16) for x, y in zip(a_p, b_hh)]
    q = [_dot(x.T.astype(BF16), y) + _dot(z.T.astype(BF16), w)
         for x, y, z, w in zip(v_p, b_hh, vh, k_hh)]
    r_p = [(r + _dot(x, y)).astype(BF16) for r, x, y in zip(r_th, a_rb, a_pb)]
    y_loc = [_dot(x, y) + _dot(z, w) for x, y, z, w in zip(a_rb, v_pb, a_rk, vhb)]

    s = [s_scr[d, h] for d, h in chains]
    sb = [x.astype(BF16) for x in s]
    y = [_dot_nt(x, z) + yl for x, z, yl in zip(r_p, sb, y_loc)]
    s_new = [x * w + _dot(z, p) + qq for x, w, z, p, qq in zip(s, w_h, sb, p_lr, q)]
    for (d, h), yy, ss in zip(chains, y, s_new):
        (yf_ref, yb_ref)[d][:, h * N:(h + 1) * N] = yy
        s_scr[d, h] = ss

    @pl.when(c == pl.num_programs(1) - 1)
    def _():
        sT_ref[:, 0] = s_scr[...]


def _rwkv_call(r, v, kk, k2, lw2, ic2, s0, *, tokens_per_batch):
    n = r.shape[0]
    nb = n // tokens_per_batch
    nc = tokens_per_batch // RWKV_CHUNK
    w = RWKV_WIDTH

    fwd = pl.BlockSpec((RWKV_CHUNK, w), lambda b, c: (b * nc + c, 0))
    bwd = pl.BlockSpec((RWKV_CHUNK, w), lambda b, c: (b * nc + nc - 1 - c, 0))

    both = [pl.BlockSpec((None, RWKV_CHUNK, w), lambda b, c: (0, b * nc + c, 0)),
            pl.BlockSpec((None, RWKV_CHUNK, w), lambda b, c: (1, b * nc + nc - 1 - c, 0))]
    st = pl.BlockSpec((2, 1, RWKV_HEADS, RWKV_HEAD, RWKV_HEAD), lambda b, c: (0, b, 0, 0, 0))
    return pl.pallas_call(
        _rwkv_kernel,
        grid=(nb, nc),
        in_specs=[fwd, fwd, fwd, bwd, bwd, bwd, *both, *both, *both, st],
        out_specs=[fwd, bwd, st],
        out_shape=[jax.ShapeDtypeStruct((n, w), F32), jax.ShapeDtypeStruct((n, w), F32),
                   jax.ShapeDtypeStruct((2, nb, RWKV_HEADS, RWKV_HEAD, RWKV_HEAD), F32)],
        scratch_shapes=[pltpu.VMEM((2, RWKV_HEADS, RWKV_HEAD, RWKV_HEAD), F32)],
        compiler_params=_cparams("parallel", "arbitrary"),
        name="rwkv",
    )(r, v, kk, r, v, kk, k2, k2, lw2, lw2, ic2, ic2, s0)


def _out_kernel(x_ref, mod_ref, ya_ref, u_ref, ybf_ref, ybb_ref, bonus_ref, gg_ref, sga_ref, sgb_ref,
                s5d_ref, wglu_ref, wproj_ref, lng_ref, lnb_ref, seg_ref, wo_ref, wout_ref, o_ref, ya_scr):
    nb, tn, _ = x_ref.shape
    rows = nb * tn

    def get(ref):
        return ref[...].reshape(rows, ref.shape[-1]).astype(F32)

    per_tile = LANE // S5_GROUP
    for cl in range(tn // S5_CHUNK):
        for q in range(S5_WIDTH // LANE):
            per_g = [ya_ref[q * per_tile + gq, cl * nb:(cl + 1) * nb, :] for gq in range(per_tile)]
            for t in range(S5_CHUNK):
                sl = slice(t * S5_GROUP, (t + 1) * S5_GROUP)
                ya_scr[q, pl.ds(cl * S5_CHUNK + t, nb, stride=tn), :] = jnp.concatenate(
                    [y[:, sl] for y in per_g], axis=1)
    ya_tok = jnp.concatenate([ya_scr[q] for q in range(S5_WIDTH // LANE)], axis=1)

    ya = jax.nn.gelu(ya_tok + s5d_ref[...] * get(u_ref))
    ya = ya * jax.nn.sigmoid(_dot(ya.astype(BF16), wglu_ref[...]))
    pa = _dot(ya.astype(BF16), wproj_ref[...])

    seg = seg_ref[...]
    yb = get(ybf_ref) + get(ybb_ref)
    mu = _dot_right01(yb, seg) * (1.0 / RWKV_HEAD)
    cen = yb - mu
    var = _dot_right01(cen * cen, seg) * (1.0 / RWKV_HEAD)
    yb = cen * lax.rsqrt(var + LN_X_EPS) * lng_ref[...] + lnb_ref[...] + get(bonus_ref)
    pb = _dot((yb * get(gg_ref)).astype(BF16), wo_ref[...])

    merged = get(sga_ref) * pa + get(sgb_ref) * pb
    out = _dot(merged.astype(BF16), wout_ref[...]).reshape(nb, tn, D_MODEL)
    o_ref[...] = x_ref[...] + mod_ref[:, 5:6] * out


def _out_call(x, mods, ya, u, ybf, ybb, bonus, gg, sga, sgb, s5d, wglu, wproj, lng, lnb, seg, wo, wout):
    nb, n, _ = x.shape
    tn = BATCH_TILE
    w = RWKV_WIDTH

    def tok(width):
        return pl.BlockSpec((nb, tn, width), lambda i: (0, i, 0))

    s5_rows = tn // S5_CHUNK * nb
    return pl.pallas_call(
        _out_kernel,
        grid=(n // tn,),
        in_specs=[tok(D_MODEL), _resident((nb, N_MOD, D_MODEL)),
                  pl.BlockSpec((S5_GROUPS, s5_rows, S5_CHUNK * S5_GROUP), lambda i: (0, i, 0)), tok(w),
                  tok(w), tok(w), tok(w), tok(w), tok(D_MODEL), tok(D_MODEL),
                  _resident((1, w)), _resident((w, w)), _resident((w, D_MODEL)), _resident((1, w)),
                  _resident((1, w)), _resident((w, w)), _resident((w, D_MODEL)), _resident((D_MODEL, D_MODEL))],
        out_specs=tok(D_MODEL),
        out_shape=jax.ShapeDtypeStruct((nb, n, D_MODEL), F32),
        scratch_shapes=[pltpu.VMEM((S5_WIDTH // LANE, nb * tn, LANE), F32)],
        compiler_params=_cparams("parallel"),
        name="outk",
    )(x, mods, ya, u, ybf, ybb, bonus, gg, sga, sgb, s5d, wglu, wproj, lng, lnb, seg, wo, wout)


def _block_diag2(a):
    z = jnp.zeros_like(a[0])
    return jnp.concatenate([jnp.concatenate([a[0], z], axis=1), jnp.concatenate([z, a[1]], axis=1)], axis=0)


def kernel(x, c, ctx, c_ctx, w_mod, b_mod, norm_g, ffn_w_gate, ffn_w_up, ffn_w_down, w_in, s5_A_re, s5_A_im, s5_log_dt, s5_B_re, s5_B_im, s5_C_re, s5_C_im, s5_D, s5_w_glu, s5_w_proj, rwkv_conv, rwkv_w0, rwkv_w2, rwkv_a0, rwkv_a2, rwkv_g2, rwkv_k_k, rwkv_k_a, rwkv_r_k, rwkv_ln_g, rwkv_ln_b, rwkv_w_o, w_out, final_g):
    nb, seq, d = x.shape
    n_ctx = ctx.shape[1]
    l = 0
    w = RWKV_WIDTH

    mod_rows = 16
    cc = jnp.concatenate([c, c_ctx[None], jnp.zeros((mod_rows - nb - 1, d), F32)], axis=0)
    mods = _mod_call(cc, w_mod[l], b_mod[l][None]).reshape(mod_rows, N_MOD, d)

    xt = x.reshape(nb * seq, d)
    ct = ctx.reshape(nb * n_ctx, d)
    row = lambda a: a.reshape(1, -1)
    bf = lambda a: a.astype(BF16)

    f1 = (row(norm_g[l, 0]), bf(ffn_w_gate[l, 0]), bf(ffn_w_up[l, 0]), bf(ffn_w_down[l, 0]), row(final_g))
    x1 = _ffn_call(xt, mods, *f1, j=0, tokens_per_batch=seq, shared_row=None, final_norm=False)
    c1 = _ffn_call(ct, mods, *f1, j=0, tokens_per_batch=n_ctx, shared_row=nb, final_norm=False)

    ip = (row(norm_g[l, 1]), bf(w_in[l]), bf(_block_diag2(rwkv_w2[l])), row(rwkv_w0[l]),
          bf(_block_diag2(rwkv_a2[l])), row(rwkv_a0[l]), bf(rwkv_g2[l]))
    mods_x = mods[:nb]
    mods_c = jnp.broadcast_to(mods[nb:nb + 1], (nb, N_MOD, d))
    u_x, us5_x, rkv_x, lw_x, ic_x, gg_x, sga_x, sgb_x = _inproj_call(x1.reshape(nb, seq, d), mods_x, *ip)
    _, us5_c, rkv_c, lw_c, ic_c, _, _, _ = _inproj_call(c1.reshape(nb, n_ctx, d), mods_c, *ip)

    def flat(a):
        return a.reshape(a.shape[:-3] + (a.shape[-3] * a.shape[-2], a.shape[-1]))

    s5p = _s5_matrices(s5_A_re[l], s5_A_im[l], s5_log_dt[l], s5_B_re[l], s5_B_im[l], s5_C_re[l], s5_C_im[l], nb)
    ya_s5 = _s5_call(us5_c, us5_x, *s5p, nb=nb)

    seg = (jnp.arange(w)[:, None] // RWKV_HEAD == jnp.arange(w)[None, :] // RWKV_HEAD).astype(BF16)
    pp = (rwkv_conv[l].reshape(9, 3 * w), row(rwkv_k_k[l]), row(rwkv_k_a[l]), row(rwkv_r_k[l]), seg)
    lw_c, ic_c, lw_x, ic_x = map(flat, (lw_c, ic_c, lw_x, ic_x))
    r_c, v_c, kk_c, k2_c, _ = _prep_call(flat(rkv_c), ic_c, *pp, tokens_per_batch=n_ctx, width=n_ctx, vertical=False)
    r_x, v_x, kk_x, k2_x, bonus = _prep_call(flat(rkv_x), ic_x, *pp, tokens_per_batch=seq, width=GRID_W, vertical=True)

    s0 = jnp.zeros((2, nb, RWKV_HEADS, RWKV_HEAD, RWKV_HEAD), F32)
    _, _, s_ctx = _rwkv_call(r_c, v_c, kk_c, k2_c, lw_c, ic_c, s0, tokens_per_batch=n_ctx)
    ybf, ybb, _ = _rwkv_call(r_x, v_x, kk_x, k2_x, lw_x, ic_x, s_ctx, tokens_per_batch=seq)

    by_batch = lambda a: a.reshape(nb, seq, a.shape[-1])
    x2 = _out_call(by_batch(x1), mods_x, ya_s5, u_x, by_batch(ybf), by_batch(ybb), by_batch(bonus), gg_x, sga_x,
                   sgb_x, row(s5_D[l]), bf(s5_w_glu[l]), bf(s5_w_proj[l]), row(rwkv_ln_g[l]), row(rwkv_ln_b[l]),
                   seg, bf(rwkv_w_o[l]), bf(w_out[l]))

    f2 = (row(norm_g[l, 2]), bf(ffn_w_gate[l, 1]), bf(ffn_w_up[l, 1]), bf(ffn_w_down[l, 1]), row(final_g))
    out = _ffn_call(flat(x2), mods, *f2, j=2, tokens_per_batch=seq, shared_row=None, final_norm=True)
    return out.reshape(nb, seq, d)
```

```python
import functools
import math

import jax
import jax.numpy as jnp
from jax import lax
from jax.experimental import pallas as pl
from jax.experimental.pallas import tpu as pltpu

F32 = jnp.float32
BF16 = jnp.bfloat16

D_MODEL = 1024
N_MOD = 9
FFN_DIM = 2816
RMS_EPS = 1e-6
GRID_W = 64
S5_WIDTH = 512
S5_GROUP = 16
S5_GROUPS = 32
S5_STATE = 64
RWKV_WIDTH = 512
RWKV_HEAD = 64
RWKV_HEADS = 8
LN_X_EPS = 64e-5
LANE = 128

S5_CHUNK = 16
RWKV_CHUNK = 64
TOKEN_TILE = 512
SMALL_TILE = 256
BATCH_TILE = 32
RWKV_STATE = (RWKV_HEADS // 2, RWKV_HEAD, 2 * RWKV_HEAD)
VMEM_LIMIT = 56 * 1024 * 1024


def _cparams(*sem):
    return pltpu.CompilerParams(dimension_semantics=sem, vmem_limit_bytes=VMEM_LIMIT)


def _resident(shape):
    nd = len(shape)
    return pl.BlockSpec(shape, lambda *_: (0,) * nd, pipeline_mode=pl.Buffered(1))


def _dot(a, b):
    return jnp.dot(a, b, preferred_element_type=F32)


def _dot_nt(a, b):
    return lax.dot_general(a, b, (((1,), (1,)), ((), ())), preferred_element_type=F32)


def _split3(x):
    hi = x.astype(BF16)
    r1 = x - hi.astype(F32)
    mid = r1.astype(BF16)
    lo = (r1 - mid.astype(F32)).astype(BF16)
    return hi, mid, lo


def _dot_left01(m01, x):
    hi, mid, lo = _split3(x)
    return _dot(m01, hi) + _dot(m01, mid) + _dot(m01, lo)


def _dot_right01(x, m01):
    hi = x.astype(BF16)
    lo = (x - hi.astype(F32)).astype(BF16)
    return _dot(hi, m01) + _dot(lo, m01)


def _rms_mod(x, g, shift, scale):
    y = x * lax.rsqrt(jnp.mean(x * x, axis=-1, keepdims=True) + RMS_EPS)
    return (y * g) * (1.0 + scale) + shift


def _mod_kernel(c_ref, w_ref, b_ref, o_ref):
    c = c_ref[...]
    s = c * jax.nn.sigmoid(c)
    o_ref[...] = jnp.dot(s, w_ref[...], precision=lax.Precision.HIGHEST, preferred_element_type=F32) + b_ref[...]


def _mod_call(cc, w_mod, b_mod):
    rows = cc.shape[0]
    return pl.pallas_call(
        _mod_kernel,
        grid=(N_MOD,),
        in_specs=[pl.BlockSpec((rows, D_MODEL), lambda j: (0, 0)),
                  pl.BlockSpec((D_MODEL, D_MODEL), lambda j: (0, j)),
                  pl.BlockSpec((1, D_MODEL), lambda j: (0, j))],
        out_specs=pl.BlockSpec((rows, D_MODEL), lambda j: (0, j)),
        out_shape=jax.ShapeDtypeStruct((rows, N_MOD * D_MODEL), F32),
        compiler_params=_cparams("arbitrary"),
        name="mod",
    )(cc, w_mod, b_mod)


def _ffn_kernel(x_ref, mod_ref, g_ref, wg_ref, wu_ref, wd_ref, fg_ref, o_ref, *, j, final_norm):
    x = x_ref[...]
    m = mod_ref[0]
    h = _rms_mod(x, g_ref[...], m[3 * j:3 * j + 1], m[3 * j + 1:3 * j + 2]).astype(BF16)
    a = _dot(h, wg_ref[...])
    a = (a * jax.nn.sigmoid(a)) * _dot(h, wu_ref[...])
    y = _dot(a.astype(BF16), wd_ref[...])
    out = x + 0.5 * m[3 * j + 2:3 * j + 3] * y
    if final_norm:
        out = out * lax.rsqrt(jnp.mean(out * out, axis=-1, keepdims=True) + RMS_EPS) * fg_ref[...]
    o_ref[...] = out


def _mod_spec(n_tiles, tiles_per_batch, shared_row):
    if shared_row is not None:
        return pl.BlockSpec((1, N_MOD, D_MODEL), lambda i: (shared_row, 0, 0))
    assert n_tiles % tiles_per_batch == 0
    return pl.BlockSpec((1, N_MOD, D_MODEL), lambda i: (i // tiles_per_batch, 0, 0))


def _ffn_call(x, mods, g, wg, wu, wd, fg, *, j, tokens_per_batch, shared_row, final_norm):
    n = x.shape[0]
    tm = min(TOKEN_TILE, tokens_per_batch)
    tok = pl.BlockSpec((tm, D_MODEL), lambda i: (i, 0))
    return pl.pallas_call(
        functools.partial(_ffn_kernel, j=j, final_norm=final_norm),
        grid=(n // tm,),
        in_specs=[tok, _mod_spec(n // tm, tokens_per_batch // tm, shared_row), _resident((1, D_MODEL)),
                  _resident((D_MODEL, FFN_DIM)), _resident((D_MODEL, FFN_DIM)), _resident((FFN_DIM, D_MODEL)),
                  _resident((1, D_MODEL))],
        out_specs=tok,
        out_shape=jax.ShapeDtypeStruct((n, D_MODEL), F32),
        compiler_params=_cparams("parallel"),
        name="ffn",
    )(x, mods, g, wg, wu, wd, fg)


O_U = S5_WIDTH
O_RKV = O_U + 3 * RWKV_WIDTH
O_WD = O_RKV + 128
O_AD = O_WD + 128
O_GD = O_AD + 128
O_GA = O_GD + D_MODEL
IN_COLS = O_GA + D_MODEL
DECAY_SCALE = math.exp(-0.5)


def _inproj_kernel(x_ref, mod_ref, g_ref, w_ref, w2_ref, w0_ref, a2_ref, a0_ref, g2_ref,
                   u_ref, us5_ref, rkv_ref, lw_ref, ic_ref, gg_ref, sga_ref, sgb_ref, u_scr):
    nb, tn, _ = x_ref.shape
    rows = nb * tn
    m = mod_ref[...]
    h = _rms_mod(x_ref[...], g_ref[...], m[:, 3:4], m[:, 4:5]).reshape(rows, D_MODEL).astype(BF16)
    proj = _dot(h, w_ref[...])

    def put(ref, val):
        ref[...] = val.reshape(nb, tn, val.shape[-1]).astype(ref.dtype)

    u = proj[:, :O_U]
    put(u_ref, u)
    put(rkv_ref, proj[:, O_U:O_RKV])
    w_log = w0_ref[...] + _dot(jnp.tanh(proj[:, O_RKV:O_WD]).astype(BF16), w2_ref[...])
    lw = -DECAY_SCALE * jax.nn.sigmoid(w_log)
    ic = jax.nn.sigmoid(a0_ref[...] + _dot(proj[:, O_WD:O_AD].astype(BF16), a2_ref[...]))
    for d in range(2):
        lw_ref[d] = lw[:, d * RWKV_WIDTH:(d + 1) * RWKV_WIDTH].reshape(nb, tn, RWKV_WIDTH)
        ic_ref[d] = ic[:, d * RWKV_WIDTH:(d + 1) * RWKV_WIDTH].reshape(nb, tn, RWKV_WIDTH)
    put(gg_ref, _dot(jax.nn.sigmoid(proj[:, O_AD:O_GD]).astype(BF16), g2_ref[...]))
    put(sga_ref, jax.nn.sigmoid(proj[:, O_GD:O_GA]))
    put(sgb_ref, jax.nn.sigmoid(proj[:, O_GA:]))

    per_tile = LANE // S5_GROUP
    for q in range(S5_WIDTH // LANE):
        u_scr[q] = u[:, q * LANE:(q + 1) * LANE]
    for q in range(S5_WIDTH // LANE):
        taps = [[u_scr[q, pl.ds(cl * S5_CHUNK + t, nb, stride=tn), :] for t in range(S5_CHUNK)]
                for cl in range(tn // S5_CHUNK)]
        for gq in range(per_tile):
            sl = slice(gq * S5_GROUP, (gq + 1) * S5_GROUP)
            rows_g = [jnp.concatenate([a[:, sl] for a in chunk_taps], axis=1) for chunk_taps in taps]
            us5_ref[q * per_tile + gq] = jnp.concatenate(rows_g, axis=0).astype(BF16)


def _inproj_call(x, mods, g, w_in, w2c, w0c, a2c, a0c, g2):
    nb, n, _ = x.shape
    tn = BATCH_TILE
    w = RWKV_WIDTH
    lw5 = S5_CHUNK * S5_GROUP

    def tok(width):
        return pl.BlockSpec((nb, tn, width), lambda i: (0, i, 0))

    tok2 = pl.BlockSpec((2, nb, tn, w), lambda i: (0, 0, i, 0))

    def shp(width, dtype=F32):
        return jax.ShapeDtypeStruct((nb, n, width), dtype)

    shp2 = jax.ShapeDtypeStruct((2, nb, n, w), F32)
    s5_rows = tn // S5_CHUNK * nb
    return pl.pallas_call(
        _inproj_kernel,
        grid=(n // tn,),
        in_specs=[tok(D_MODEL), _resident((nb, N_MOD, D_MODEL)), _resident((1, D_MODEL)),
                  _resident((D_MODEL, IN_COLS)), _resident((128, 2 * RWKV_WIDTH)), _resident((1, 2 * RWKV_WIDTH)),
                  _resident((128, 2 * RWKV_WIDTH)), _resident((1, 2 * RWKV_WIDTH)), _resident((128, RWKV_WIDTH))],
        out_specs=[tok(S5_WIDTH), pl.BlockSpec((S5_GROUPS, s5_rows, lw5), lambda i: (0, i, 0)), tok(3 * w),
                   tok2, tok2, tok(w), tok(D_MODEL), tok(D_MODEL)],
        out_shape=[shp(S5_WIDTH), jax.ShapeDtypeStruct((S5_GROUPS, n // S5_CHUNK * nb, lw5), BF16), shp(3 * w),
                   shp2, shp2, shp(w), shp(D_MODEL, BF16), shp(D_MODEL, BF16)],
        scratch_shapes=[pltpu.VMEM((S5_WIDTH // LANE, nb * tn, LANE), F32)],
        compiler_params=_cparams("parallel"),
        name="inproj",
    )(x, mods, g, w_in, w2c, w0c, a2c, a0c, g2)


def _prep_kernel(up_ref, mid_ref, dn_ref, ic_ref, cw_ref, kk_w_ref, ka_ref, rk_ref, seg_ref,
                 r_ref, v_ref, kk_ref, k2_ref, bonus_ref, *, width, vertical, tiles_per_image):
    tm = mid_ref.shape[0]
    cw = cw_ref[...]
    mid = mid_ref[...]
    xpos = lax.broadcasted_iota(jnp.int32, (tm, 1), 0) % width
    not_first = xpos != 0
    not_last = xpos != width - 1

    rows = [(mid, 1)]
    if vertical:
        t = pl.program_id(0) % tiles_per_image
        up = jnp.where(t == 0, 0.0, up_ref[...])
        dn = jnp.where(t == tiles_per_image - 1, 0.0, dn_ref[...])
        ext = jnp.concatenate([up, mid, dn], axis=0)
        rows += [(ext[0:tm], 0), (ext[2 * width:2 * width + tm], 2)]

    def column(dx):
        return sum(base * cw[3 * dy + dx:3 * dy + dx + 1] for base, dy in rows)

    acc = (column(1) + jnp.where(not_first, pltpu.roll(column(0), 1, 0), 0.0)
           + jnp.where(not_last, pltpu.roll(column(2), tm - 1, 0), 0.0))

    w = RWKV_WIDTH
    r, k, v = acc[:, :w], acc[:, w:2 * w], acc[:, 2 * w:]
    seg = seg_ref[...]
    kk = k * kk_w_ref[...]
    kk = kk * lax.rsqrt(_dot_right01(kk * kk, seg) + 1e-12)
    ka = ka_ref[...]
    rrk = r * rk_ref[...]
    kf = k * (1.0 + (ic_ref[0] - 1.0) * ka)
    kb = k * (1.0 + (ic_ref[1] - 1.0) * ka)
    r_ref[...] = r
    v_ref[...] = v
    kk_ref[...] = kk
    k2_ref[0] = kf
    k2_ref[1] = kb
    bonus_ref[...] = _dot_right01(rrk * kf + rrk * kb, seg) * v


def _prep_call(rkv, iclr, cw, kk_w, ka, rk, seg, *, tokens_per_batch, width, vertical):
    n = rkv.shape[0]
    tm = min(SMALL_TILE, tokens_per_batch)
    rows_per_tile = tm // width
    n_rows = n // width
    w3 = 3 * RWKV_WIDTH
    w = RWKV_WIDTH
    tok = pl.BlockSpec((tm, w), lambda i: (i, 0))
    return pl.pallas_call(
        functools.partial(_prep_kernel, width=width, vertical=vertical, tiles_per_image=tokens_per_batch // tm),
        grid=(n // tm,),
        in_specs=[pl.BlockSpec((width, w3), lambda i: (jnp.maximum(i * rows_per_tile - 1, 0), 0)),
                  pl.BlockSpec((tm, w3), lambda i: (i, 0)),
                  pl.BlockSpec((width, w3), lambda i: (jnp.minimum((i + 1) * rows_per_tile, n_rows - 1), 0)),
                  pl.BlockSpec((2, tm, w), lambda i: (0, i, 0)),
                  _resident((9, w3)), _resident((1, w)), _resident((1, w)), _resident((1, w)), _resident((w, w))],
        out_specs=[tok, tok, tok, pl.BlockSpec((2, tm, w), lambda i: (0, i, 0)), tok],
        out_shape=[jax.ShapeDtypeStruct((n, w), F32)] * 3 + [jax.ShapeDtypeStruct((2, n, w), F32),
                                                              jax.ShapeDtypeStruct((n, w), F32)],
        compiler_params=_cparams("parallel"),
        name="prep",
    )(rkv, rkv, rkv, iclr, cw, kk_w, ka, rk, seg)


def _s5_kernel(uc_ref, ux_ref, wcat_ref, tfb_ref, cp_ref, ab_ref, y_ref, x_scr, h_scr, *, ctx_chunks, x_chunks, nb):
    ux = ux_ref[0]
    r0 = ctx_chunks * nb
    x_scr[:r0] = _dot(uc_ref[0], wcat_ref[0])
    x_scr[r0:] = _dot(ux, wcat_ref[0])
    ab = ab_ref[0]
    total = ctx_chunks + x_chunks

    def scan(carry, first, count, step, a1, a2, col):
        def body(i, hc):
            h, hs = hc
            off = pl.multiple_of((first + i * step) * nb, nb)
            h_scr[pl.ds(off, nb), col * 128:(col + 1) * 128] = h
            x = x_scr[pl.ds(off, nb), 2 * col * 128:(2 * col + 1) * 128]
            xs = x_scr[pl.ds(off, nb), (2 * col + 1) * 128:(2 * col + 2) * 128]
            return a1 * h + a2 * hs + x, a1 * hs - a2 * h + xs
        return lax.fori_loop(0, count, body, carry)

    zero = jnp.zeros((nb, 128), F32)
    scan((zero, zero), 0, total, 1, ab[0], ab[1], 0)
    carry = scan((zero, zero), ctx_chunks - 1, ctx_chunks, -1, ab[2], ab[3], 1)
    scan(carry, total - 1, x_chunks, -1, ab[2], ab[3], 1)

    y_ref[0] = _dot(ux, tfb_ref[0]) + _dot(h_scr[r0:, :].astype(BF16), cp_ref[0])


def _s5_call(u_c, u_x, wcat, tfb, cp, ab, *, nb):
    g = u_x.shape[0]
    ctx_chunks = u_c.shape[1] // nb
    x_chunks = u_x.shape[1] // nb
    rows = (ctx_chunks + x_chunks) * nb
    lw = S5_CHUNK * S5_GROUP
    return pl.pallas_call(
        functools.partial(_s5_kernel, ctx_chunks=ctx_chunks, x_chunks=x_chunks, nb=nb),
        grid=(g,),
        in_specs=[pl.BlockSpec((1, ctx_chunks * nb, lw), lambda i: (i, 0, 0)),
                  pl.BlockSpec((1, x_chunks * nb, lw), lambda i: (i, 0, 0)),
                  pl.BlockSpec((1, lw, 512), lambda i: (i, 0, 0)),
                  pl.BlockSpec((1, lw, lw), lambda i: (i, 0, 0)),
                  pl.BlockSpec((1, 256, lw), lambda i: (i, 0, 0)),
                  pl.BlockSpec((1, 4, nb, 128), lambda i: (i, 0, 0, 0))],
        out_specs=pl.BlockSpec((1, x_chunks * nb, lw), lambda i: (i, 0, 0)),
        out_shape=jax.ShapeDtypeStruct((g, x_chunks * nb, lw), F32),
        scratch_shapes=[pltpu.VMEM((rows, 512), F32), pltpu.VMEM((rows, 256), F32)],
        compiler_params=_cparams("parallel"),
        name="s5",
    )(u_c, u_x, wcat, tfb, cp, ab)


def _s5_matrices(lam_re, lam_im, log_dt, b_re, b_im, c_re, c_im, nb):
    L, P, GS = S5_CHUNK, S5_STATE, S5_GROUP
    hp = lax.Precision.HIGHEST
    dt = jnp.exp(log_dt)[..., None]
    ab_re = jnp.exp(dt * lam_re) * jnp.cos(dt * lam_im)
    ab_im = jnp.exp(dt * lam_re) * jnp.sin(dt * lam_im)
    den = lam_re * lam_re + lam_im * lam_im
    z_re = ((ab_re - 1.0) * lam_re + ab_im * lam_im) / den
    z_im = (ab_im * lam_re - (ab_re - 1.0) * lam_im) / den
    bb_re = z_re[..., None] * b_re - z_im[..., None] * b_im
    bb_im = z_re[..., None] * b_im + z_im[..., None] * b_re
    k = jnp.arange(L + 1, dtype=F32)[:, None, None, None]
    pw_re = jnp.exp(k * dt * lam_re) * jnp.cos(k * dt * lam_im)
    pw_im = jnp.exp(k * dt * lam_re) * jnp.sin(k * dt * lam_im)
    ca_re = c_re[None] * pw_re[:, :, :, None, :] - c_im[None] * pw_im[:, :, :, None, :]
    ca_im = c_re[None] * pw_im[:, :, :, None, :] + c_im[None] * pw_re[:, :, :, None, :]
    ab_re_k = pw_re[..., None] * bb_re[None] - pw_im[..., None] * bb_im[None]
    ab_im_k = pw_re[..., None] * bb_im[None] + pw_im[..., None] * bb_re[None]
    kern = (jnp.einsum('kdgip,dgpj->dgkij', ca_re[:L], bb_re, precision=hp)
            - jnp.einsum('kdgip,dgpj->dgkij', ca_im[:L], bb_im, precision=hp))

    s = jnp.arange(L)[:, None]
    t = jnp.arange(L)[None, :]

    def toeplitz(kd, lag, valid):
        pick = ((lag[:, :, None] == jnp.arange(L)) & valid[:, :, None]).astype(F32)
        return jnp.einsum('stk,gkij->gsjti', pick, kd, precision=hp).reshape(-1, L * GS, L * GS)

    tfb = toeplitz(kern[0], t - s, t >= s) + toeplitz(kern[1], s - t, s >= t)

    def state_in(d, powers):
        wr = ab_re_k[powers, d].transpose(1, 0, 3, 2).reshape(-1, L * GS, P)
        wi = ab_im_k[powers, d].transpose(1, 0, 3, 2).reshape(-1, L * GS, P)
        return jnp.concatenate([wr, wi, wi, wr], axis=-1)

    wcat = jnp.concatenate([state_in(0, L - 1 - jnp.arange(L)), state_in(1, jnp.arange(L))], axis=-1)

    def state_out(d, powers):
        cr = ca_re[powers, d].transpose(1, 3, 0, 2).reshape(-1, P, L * GS)
        ci = ca_im[powers, d].transpose(1, 3, 0, 2).reshape(-1, P, L * GS)
        return jnp.concatenate([cr, -ci], axis=1)

    cp = jnp.concatenate([state_out(0, 1 + jnp.arange(L)), state_out(1, L - jnp.arange(L))], axis=1)

    def step_coeffs(d):
        a1 = jnp.concatenate([pw_re[L, d], pw_re[L, d]], axis=-1)
        a2 = jnp.concatenate([-pw_im[L, d], pw_im[L, d]], axis=-1)
        return [a1, a2]

    ab = jnp.stack(step_coeffs(0) + step_coeffs(1), axis=1)
    ab = jnp.broadcast_to(ab[:, :, None, :], ab.shape[:2] + (nb, 128))
    return wcat.astype(BF16), tfb.astype(BF16), cp.astype(BF16), ab


def _rwkv_kernel(rf_ref, vf_ref, kkf_ref, rb_ref, vb_ref, kkb_ref, kf_ref, kb_ref, lwf_ref, lwb_ref,
                 icf_ref, icb_ref, s0_ref, yf_ref, yb_ref, sT_ref, s_scr):
    c = pl.program_id(1)
    L, N = RWKV_CHUNK, RWKV_HEAD
    P = 2 * N
    n_pairs = RWKV_WIDTH // P

    @pl.when(c == 0)
    def _():
        s_scr[...] = s0_ref[:, 0]

    row = lax.broadcasted_iota(jnp.int32, (L, P), 0)
    lane = lax.broadcasted_iota(jnp.int32, (L, P), 1)
    col = lane % N
    first = lane < N
    eye = (col == row).astype(F32)
    strict = (col < row, col > row)
    incl = (col <= row, col >= row)
    row1 = lax.broadcasted_iota(jnp.int32, (L, L), 0)
    col1 = lax.broadcasted_iota(jnp.int32, (L, L), 1)
    tri = (col1 <= row1, col1 >= row1)

    def bd(y):
        return jnp.concatenate([jnp.where(first, y, 0).astype(BF16), jnp.where(first, 0, y).astype(BF16)], axis=0)

    def pmm(x, y):
        return _dot(x.astype(BF16), bd(y))

    def pmm_tn(x, y):
        full = _dot(x.T.astype(BF16), y.astype(BF16))
        return jnp.where(first, full[:N], full[N:])

    chains = [(d, p) for d in range(2) for p in range(n_pairs)]
    a_t, r_t, v_f, b_t, k_t, b_h, k_h, w_tot = [], [], [], [], [], [], [], []
    per_dir = ((rf_ref, vf_ref, kkf_ref, kf_ref, lwf_ref, icf_ref), (rb_ref, vb_ref, kkb_ref, kb_ref, lwb_ref, icb_ref))
    for d, (r_ref, v_ref, kk_ref, k_ref, lw_ref, ic_ref) in enumerate(per_dir):
        lw = lw_ref[...]
        cum = _dot_left01(jnp.where(tri[d], 1.0, 0.0).astype(BF16), lw)
        tot = jnp.sum(lw, axis=0, keepdims=True)
        kk = kk_ref[...]
        b = kk * ic_ref[...]
        k = k_ref[...]
        e_ninc = jnp.exp(-cum)
        e_rem = jnp.exp(tot - cum)
        a_t.append(-kk * jnp.exp(cum - lw))
        r_t.append(r_ref[...] * jnp.exp(cum))
        v_f.append(v_ref[...])
        b_t.append((b * e_ninc).astype(BF16))
        k_t.append((k * e_ninc).astype(BF16))
        b_h.append((b * e_rem).astype(BF16))
        k_h.append((k * e_rem).astype(BF16))
        w_tot.append(jnp.exp(tot))

    def pair(xs):
        return [xs[d][:, p * P:(p + 1) * P] for d, p in chains]

    a_tp, r_tp, vp, b_tp, k_tp, b_hp, k_hp, w_p = map(pair, (a_t, r_t, v_f, b_t, k_t, b_h, k_h, w_tot))
    def rows2(x, y):
        return jnp.concatenate([x, y], axis=0)

    def cols2(x, y):
        return jnp.concatenate([x, y], axis=1)

    ar = [rows2(a, r).astype(BF16) for a, r in zip(a_tp, r_tp)]
    gram = [_dot_nt(x, rows2(bd(y), bd(z))) for x, y, z in zip(ar, b_tp, k_tp)]
    n_ab = [jnp.where(strict[d], g[:L, :P], 0.0) for (d, _), g in zip(chains, gram)]
    a_ak = [jnp.where(strict[d], g[:L, P:], 0.0) for (d, _), g in zip(chains, gram)]
    a_rb = [jnp.where(incl[d], g[L:, :P], 0.0).astype(BF16) for (d, _), g in zip(chains, gram)]
    a_rk = [jnp.where(incl[d], g[L:, P:], 0.0) for (d, _), g in zip(chains, gram)]
    kv = [pmm(rows2(x, y), z) for x, y, z in zip(a_ak, a_rk, vp)]
    inv = [eye + x for x in n_ab]
    pw = [pmm(x, x) for x in n_ab]
    for _ in range(4):
        both = [pmm(rows2(x, t), x) for x, t in zip(pw, inv)]
        inv = [t + z[L:] for t, z in zip(inv, both)]
        pw = [z[:L] for z in both]
    inv = [t + pmm(t, x) for t, x in zip(inv, pw)]
    invb = [x.astype(BF16) for x in inv]
    apvp = [_dot(t, cols2(bd(a), bd(z[:L]))) for t, a, z in zip(invb, a_tp, kv)]
    a_p = [z[:, :P] for z in apvp]
    v_p = [z[:, P:] for z in apvp]
    rb = [_dot(x, cols2(bd(y), bd(z))) for x, y, z in zip(a_rb, a_p, v_p)]
    r_p = [r + z[:, :P] for r, z in zip(r_tp, rb)]
    y_loc = [z[:, P:] + w[L:] for z, w in zip(rb, kv)]
    tn = [_dot(z.T.astype(BF16), y) for z, y in zip(apvp, b_hp)]
    p_lr = [jnp.where(first, z[:N], z[N:P]) for z in tn]
    q = [jnp.where(first, z[P:P + N], z[P + N:]) + pmm_tn(x, y) for z, x, y in zip(tn, vp, k_hp)]

    s = [s_scr[d, p] for d, p in chains]
    s_bd = [bd(x) for x in s]
    y = [_dot_nt(x.astype(BF16), z) + yl for x, z, yl in zip(r_p, s_bd, y_loc)]
    s_new = [x * w + pmm(x, pl_) + qq for x, w, pl_, qq in zip(s, w_p, p_lr, q)]
    for (d, p), yy, ss in zip(chains, y, s_new):
        (yf_ref, yb_ref)[d][:, p * P:(p + 1) * P] = yy
        s_scr[d, p] = ss

    @pl.when(c == pl.num_programs(1) - 1)
    def _():
        sT_ref[:, 0] = s_scr[...]


def _rwkv_call(r, v, kk, k2, lw2, ic2, s0, *, tokens_per_batch):
    n = r.shape[0]
    nb = n // tokens_per_batch
    nc = tokens_per_batch // RWKV_CHUNK
    w = RWKV_WIDTH

    fwd = pl.BlockSpec((RWKV_CHUNK, w), lambda b, c: (b * nc + c, 0))
    bwd = pl.BlockSpec((RWKV_CHUNK, w), lambda b, c: (b * nc + nc - 1 - c, 0))

    both = [pl.BlockSpec((None, RWKV_CHUNK, w), lambda b, c: (0, b * nc + c, 0)),
            pl.BlockSpec((None, RWKV_CHUNK, w), lambda b, c: (1, b * nc + nc - 1 - c, 0))]
    st = pl.BlockSpec((2, 1) + RWKV_STATE, lambda b, c: (0, b, 0, 0, 0))
    return pl.pallas_call(
        _rwkv_kernel,
        grid=(nb, nc),
        in_specs=[fwd, fwd, fwd, bwd, bwd, bwd, *both, *both, *both, st],
        out_specs=[fwd, bwd, st],
        out_shape=[jax.ShapeDtypeStruct((n, w), F32), jax.ShapeDtypeStruct((n, w), F32),
                   jax.ShapeDtypeStruct((2, nb) + RWKV_STATE, F32)],
        scratch_shapes=[pltpu.VMEM((2,) + RWKV_STATE, F32)],
        compiler_params=_cparams("parallel", "arbitrary"),
        name="rwkv",
    )(r, v, kk, r, v, kk, k2, k2, lw2, lw2, ic2, ic2, s0)


def _out_kernel(x_ref, mod_ref, ya_ref, u_ref, ybf_ref, ybb_ref, bonus_ref, gg_ref, sga_ref, sgb_ref,
                s5d_ref, wglu_ref, wproj_ref, lng_ref, lnb_ref, seg_ref, wo_ref, wout_ref, o_ref, ya_scr):
    nb, tn, _ = x_ref.shape
    rows = nb * tn

    def get(ref):
        return ref[...].reshape(rows, ref.shape[-1]).astype(F32)

    per_tile = LANE // S5_GROUP
    for cl in range(tn // S5_CHUNK):
        for q in range(S5_WIDTH // LANE):
            per_g = [ya_ref[q * per_tile + gq, cl * nb:(cl + 1) * nb, :] for gq in range(per_tile)]
            for t in range(S5_CHUNK):
                sl = slice(t * S5_GROUP, (t + 1) * S5_GROUP)
                ya_scr[q, pl.ds(cl * S5_CHUNK + t, nb, stride=tn), :] = jnp.concatenate(
                    [y[:, sl] for y in per_g], axis=1)
    ya_tok = jnp.concatenate([ya_scr[q] for q in range(S5_WIDTH // LANE)], axis=1)

    ya = jax.nn.gelu(ya_tok + s5d_ref[...] * get(u_ref))
    ya = ya * jax.nn.sigmoid(_dot(ya.astype(BF16), wglu_ref[...]))
    pa = _dot(ya.astype(BF16), wproj_ref[...])

    seg = seg_ref[...]
    yb = get(ybf_ref) + get(ybb_ref)
    mu = _dot_right01(yb, seg) * (1.0 / RWKV_HEAD)
    cen = yb - mu
    var = _dot_right01(cen * cen, seg) * (1.0 / RWKV_HEAD)
    yb = cen * lax.rsqrt(var + LN_X_EPS) * lng_ref[...] + lnb_ref[...] + get(bonus_ref)
    pb = _dot((yb * get(gg_ref)).astype(BF16), wo_ref[...])

    merged = get(sga_ref) * pa + get(sgb_ref) * pb
    out = _dot(merged.astype(BF16), wout_ref[...]).reshape(nb, tn, D_MODEL)
    o_ref[...] = x_ref[...] + mod_ref[:, 5:6] * out


def _out_call(x, mods, ya, u, ybf, ybb, bonus, gg, sga, sgb, s5d, wglu, wproj, lng, lnb, seg, wo, wout):
    nb, n, _ = x.shape
    tn = BATCH_TILE
    w = RWKV_WIDTH

    def tok(width):
        return pl.BlockSpec((nb, tn, width), lambda i: (0, i, 0))

    s5_rows = tn // S5_CHUNK * nb
    return pl.pallas_call(
        _out_kernel,
        grid=(n // tn,),
        in_specs=[tok(D_MODEL), _resident((nb, N_MOD, D_MODEL)),
                  pl.BlockSpec((S5_GROUPS, s5_rows, S5_CHUNK * S5_GROUP), lambda i: (0, i, 0)), tok(w),
                  tok(w), tok(w), tok(w), tok(w), tok(D_MODEL), tok(D_MODEL),
                  _resident((1, w)), _resident((w, w)), _resident((w, D_MODEL)), _resident((1, w)),
                  _resident((1, w)), _resident((w, w)), _resident((w, D_MODEL)), _resident((D_MODEL, D_MODEL))],
        out_specs=tok(D_MODEL),
        out_shape=jax.ShapeDtypeStruct((nb, n, D_MODEL), F32),
        scratch_shapes=[pltpu.VMEM((S5_WIDTH // LANE, nb * tn, LANE), F32)],
        compiler_params=_cparams("parallel"),
        name="outk",
    )(x, mods, ya, u, ybf, ybb, bonus, gg, sga, sgb, s5d, wglu, wproj, lng, lnb, seg, wo, wout)


def _block_diag2(a):
    z = jnp.zeros_like(a[0])
    return jnp.concatenate([jnp.concatenate([a[0], z], axis=1), jnp.concatenate([z, a[1]], axis=1)], axis=0)


def kernel(x, c, ctx, c_ctx, w_mod, b_mod, norm_g, ffn_w_gate, ffn_w_up, ffn_w_down, w_in, s5_A_re, s5_A_im, s5_log_dt, s5_B_re, s5_B_im, s5_C_re, s5_C_im, s5_D, s5_w_glu, s5_w_proj, rwkv_conv, rwkv_w0, rwkv_w2, rwkv_a0, rwkv_a2, rwkv_g2, rwkv_k_k, rwkv_k_a, rwkv_r_k, rwkv_ln_g, rwkv_ln_b, rwkv_w_o, w_out, final_g):
    nb, seq, d = x.shape
    n_ctx = ctx.shape[1]
    l = 0
    w = RWKV_WIDTH

    mod_rows = 16
    cc = jnp.concatenate([c, c_ctx[None], jnp.zeros((mod_rows - nb - 1, d), F32)], axis=0)
    mods = _mod_call(cc, w_mod[l], b_mod[l][None]).reshape(mod_rows, N_MOD, d)

    xt = x.reshape(nb * seq, d)
    ct = ctx.reshape(nb * n_ctx, d)
    row = lambda a: a.reshape(1, -1)
    bf = lambda a: a.astype(BF16)

    f1 = (row(norm_g[l, 0]), bf(ffn_w_gate[l, 0]), bf(ffn_w_up[l, 0]), bf(ffn_w_down[l, 0]), row(final_g))
    x1 = _ffn_call(xt, mods, *f1, j=0, tokens_per_batch=seq, shared_row=None, final_norm=False)
    c1 = _ffn_call(ct, mods, *f1, j=0, tokens_per_batch=n_ctx, shared_row=nb, final_norm=False)

    ip = (row(norm_g[l, 1]), bf(w_in[l]), bf(_block_diag2(rwkv_w2[l])), row(rwkv_w0[l]),
          bf(_block_diag2(rwkv_a2[l])), row(rwkv_a0[l]), bf(rwkv_g2[l]))
    mods_x = mods[:nb]
    mods_c = jnp.broadcast_to(mods[nb:nb + 1], (nb, N_MOD, d))
    u_x, us5_x, rkv_x, lw_x, ic_x, gg_x, sga_x, sgb_x = _inproj_call(x1.reshape(nb, seq, d), mods_x, *ip)
    _, us5_c, rkv_c, lw_c, ic_c, _, _, _ = _inproj_call(c1.reshape(nb, n_ctx, d), mods_c, *ip)

    def flat(a):
        return a.reshape(a.shape[:-3] + (a.shape[-3] * a.shape[-2], a.shape[-1]))

    s5p = _s5_matrices(s5_A_re[l], s5_A_im[l], s5_log_dt[l], s5_B_re[l], s5_B_im[l], s5_C_re[l], s5_C_im[l], nb)
    ya_s5 = _s5_call(us5_c, us5_x, *s5p, nb=nb)

    seg = (jnp.arange(w)[:, None] // RWKV_HEAD == jnp.arange(w)[None, :] // RWKV_HEAD).astype(BF16)
    pp = (rwkv_conv[l].reshape(9, 3 * w), row(rwkv_k_k[l]), row(rwkv_k_a[l]), row(rwkv_r_k[l]), seg)
    lw_c, ic_c, lw_x, ic_x = map(flat, (lw_c, ic_c, lw_x, ic_x))
    r_c, v_c, kk_c, k2_c, _ = _prep_call(flat(rkv_c), ic_c, *pp, tokens_per_batch=n_ctx, width=n_ctx, vertical=False)
    r_x, v_x, kk_x, k2_x, bonus = _prep_call(flat(rkv_x), ic_x, *pp, tokens_per_batch=seq, width=GRID_W, vertical=True)

    s0 = jnp.zeros((2, nb) + RWKV_STATE, F32)
    _, _, s_ctx = _rwkv_call(r_c, v_c, kk_c, k2_c, lw_c, ic_c, s0, tokens_per_batch=n_ctx)
    ybf, ybb, _ = _rwkv_call(r_x, v_x, kk_x, k2_x, lw_x, ic_x, s_ctx, tokens_per_batch=seq)

    by_batch = lambda a: a.reshape(nb, seq, a.shape[-1])
    x2 = _out_call(by_batch(x1), mods_x, ya_s5, u_x, by_batch(ybf), by_batch(ybb), by_batch(bonus), gg_x, sga_x,
                   sgb_x, row(s5_D[l]), bf(s5_w_glu[l]), bf(s5_w_proj[l]), row(rwkv_ln_g[l]), row(rwkv_ln_b[l]),
                   seg, bf(rwkv_w_o[l]), bf(w_out[l]))

    f2 = (row(norm_g[l, 2]), bf(ffn_w_gate[l, 1]), bf(ffn_w_up[l, 1]), bf(ffn_w_down[l, 1]), row(final_g))
    out = _ffn_call(flat(x2), mods, *f2, j=2, tokens_per_batch=seq, shared_row=None, final_norm=True)
    return out.reshape(nb, seq, d)
```

```python
import functools
import math

import jax
import jax.numpy as jnp
from jax import lax
from jax.experimental import pallas as pl
from jax.experimental.pallas import tpu as pltpu

F32 = jnp.float32
BF16 = jnp.bfloat16

D_MODEL = 1024
N_MOD = 9
FFN_DIM = 2816
RMS_EPS = 1e-6
GRID_W = 64
S5_WIDTH = 512
S5_GROUP = 16
S5_GROUPS = 32
S5_STATE = 64
RWKV_WIDTH = 512
RWKV_HEAD = 64
RWKV_HEADS = 8
LN_X_EPS = 64e-5
LANE = 128

S5_CHUNK = 16
RWKV_CHUNK = 64
TOKEN_TILE = 512
SMALL_TILE = 256
BATCH_TILE = 32
RWKV_STATE = (RWKV_HEADS // 2, RWKV_HEAD, 2 * RWKV_HEAD)
RWKV_BATCH = 2
VMEM_LIMIT = 56 * 1024 * 1024


def _cparams(*sem):
    return pltpu.CompilerParams(dimension_semantics=sem, vmem_limit_bytes=VMEM_LIMIT)


def _resident(shape):
    nd = len(shape)
    return pl.BlockSpec(shape, lambda *_: (0,) * nd, pipeline_mode=pl.Buffered(1))


def _dot(a, b):
    return jnp.dot(a, b, preferred_element_type=F32)


def _dot_nt(a, b):
    return lax.dot_general(a, b, (((1,), (1,)), ((), ())), preferred_element_type=F32)


def _split3(x):
    hi = x.astype(BF16)
    r1 = x - hi.astype(F32)
    mid = r1.astype(BF16)
    lo = (r1 - mid.astype(F32)).astype(BF16)
    return hi, mid, lo


def _dot_left01(m01, x):
    hi, mid, lo = _split3(x)
    return _dot(m01, hi) + _dot(m01, mid) + _dot(m01, lo)


def _dot_right01(x, m01):
    hi = x.astype(BF16)
    lo = (x - hi.astype(F32)).astype(BF16)
    return _dot(hi, m01) + _dot(lo, m01)


def _rms_mod(x, g, shift, scale):
    y = x * lax.rsqrt(jnp.mean(x * x, axis=-1, keepdims=True) + RMS_EPS)
    return (y * g) * (1.0 + scale) + shift


def _mod_kernel(c_ref, w_ref, b_ref, o_ref):
    c = c_ref[...]
    s = c * jax.nn.sigmoid(c)
    o_ref[...] = jnp.dot(s, w_ref[...], precision=lax.Precision.HIGHEST, preferred_element_type=F32) + b_ref[...]


def _mod_call(cc, w_mod, b_mod):
    rows = cc.shape[0]
    return pl.pallas_call(
        _mod_kernel,
        grid=(N_MOD,),
        in_specs=[pl.BlockSpec((rows, D_MODEL), lambda j: (0, 0)),
                  pl.BlockSpec((D_MODEL, D_MODEL), lambda j: (0, j)),
                  pl.BlockSpec((1, D_MODEL), lambda j: (0, j))],
        out_specs=pl.BlockSpec((rows, D_MODEL), lambda j: (0, j)),
        out_shape=jax.ShapeDtypeStruct((rows, N_MOD * D_MODEL), F32),
        compiler_params=_cparams("arbitrary"),
        name="mod",
    )(cc, w_mod, b_mod)


def _ffn_kernel(x_ref, mod_ref, g_ref, wg_ref, wu_ref, wd_ref, fg_ref, o_ref, *, j, final_norm):
    x = x_ref[...]
    m = mod_ref[0]
    h = _rms_mod(x, g_ref[...], m[3 * j:3 * j + 1], m[3 * j + 1:3 * j + 2]).astype(BF16)
    a = _dot(h, wg_ref[...])
    a = (a * jax.nn.sigmoid(a)) * _dot(h, wu_ref[...])
    y = _dot(a.astype(BF16), wd_ref[...])
    out = x + 0.5 * m[3 * j + 2:3 * j + 3] * y
    if final_norm:
        out = out * lax.rsqrt(jnp.mean(out * out, axis=-1, keepdims=True) + RMS_EPS) * fg_ref[...]
    o_ref[...] = out


def _mod_spec(n_tiles, tiles_per_batch, shared_row):
    if shared_row is not None:
        return pl.BlockSpec((1, N_MOD, D_MODEL), lambda i: (shared_row, 0, 0))
    assert n_tiles % tiles_per_batch == 0
    return pl.BlockSpec((1, N_MOD, D_MODEL), lambda i: (i // tiles_per_batch, 0, 0))


def _ffn_call(x, mods, g, wg, wu, wd, fg, *, j, tokens_per_batch, shared_row, final_norm):
    n = x.shape[0]
    tm = min(TOKEN_TILE, tokens_per_batch)
    tok = pl.BlockSpec((tm, D_MODEL), lambda i: (i, 0))
    return pl.pallas_call(
        functools.partial(_ffn_kernel, j=j, final_norm=final_norm),
        grid=(n // tm,),
        in_specs=[tok, _mod_spec(n // tm, tokens_per_batch // tm, shared_row), _resident((1, D_MODEL)),
                  _resident((D_MODEL, FFN_DIM)), _resident((D_MODEL, FFN_DIM)), _resident((FFN_DIM, D_MODEL)),
                  _resident((1, D_MODEL))],
        out_specs=tok,
        out_shape=jax.ShapeDtypeStruct((n, D_MODEL), F32),
        compiler_params=_cparams("parallel"),
        name="ffn",
    )(x, mods, g, wg, wu, wd, fg)


O_U = S5_WIDTH
O_RKV = O_U + 3 * RWKV_WIDTH
O_WD = O_RKV + 128
O_AD = O_WD + 128
O_GD = O_AD + 128
O_GA = O_GD + D_MODEL
IN_COLS = O_GA + D_MODEL
DECAY_SCALE = math.exp(-0.5)


def _inproj_kernel(x_ref, mod_ref, g_ref, w_ref, w2_ref, w0_ref, a2_ref, a0_ref, g2_ref,
                   u_ref, us5_ref, rkv_ref, lw_ref, ic_ref, gg_ref, sga_ref, sgb_ref, u_scr):
    nb, tn, _ = x_ref.shape
    rows = nb * tn
    m = mod_ref[...]
    h = _rms_mod(x_ref[...], g_ref[...], m[:, 3:4], m[:, 4:5]).reshape(rows, D_MODEL).astype(BF16)
    proj = _dot(h, w_ref[...])

    def put(ref, val):
        ref[...] = val.reshape(nb, tn, val.shape[-1]).astype(ref.dtype)

    u = proj[:, :O_U]
    put(u_ref, u)
    put(rkv_ref, proj[:, O_U:O_RKV])
    w_log = w0_ref[...] + _dot(jnp.tanh(proj[:, O_RKV:O_WD]).astype(BF16), w2_ref[...])
    lw = -DECAY_SCALE * jax.nn.sigmoid(w_log)
    ic = jax.nn.sigmoid(a0_ref[...] + _dot(proj[:, O_WD:O_AD].astype(BF16), a2_ref[...]))
    for d in range(2):
        lw_ref[d] = lw[:, d * RWKV_WIDTH:(d + 1) * RWKV_WIDTH].reshape(nb, tn, RWKV_WIDTH)
        ic_ref[d] = ic[:, d * RWKV_WIDTH:(d + 1) * RWKV_WIDTH].reshape(nb, tn, RWKV_WIDTH)
    put(gg_ref, _dot(jax.nn.sigmoid(proj[:, O_AD:O_GD]).astype(BF16), g2_ref[...]))
    put(sga_ref, jax.nn.sigmoid(proj[:, O_GD:O_GA]))
    put(sgb_ref, jax.nn.sigmoid(proj[:, O_GA:]))

    per_tile = LANE // S5_GROUP
    for q in range(S5_WIDTH // LANE):
        u_scr[q] = u[:, q * LANE:(q + 1) * LANE]
    for q in range(S5_WIDTH // LANE):
        taps = [[u_scr[q, pl.ds(cl * S5_CHUNK + t, nb, stride=tn), :] for t in range(S5_CHUNK)]
                for cl in range(tn // S5_CHUNK)]
        for gq in range(per_tile):
            sl = slice(gq * S5_GROUP, (gq + 1) * S5_GROUP)
            rows_g = [jnp.concatenate([a[:, sl] for a in chunk_taps], axis=1) for chunk_taps in taps]
            us5_ref[q * per_tile + gq] = jnp.concatenate(rows_g, axis=0).astype(BF16)


def _inproj_call(x, mods, g, w_in, w2c, w0c, a2c, a0c, g2):
    nb, n, _ = x.shape
    tn = BATCH_TILE
    w = RWKV_WIDTH
    lw5 = S5_CHUNK * S5_GROUP

    def tok(width):
        return pl.BlockSpec((nb, tn, width), lambda i: (0, i, 0))

    tok2 = pl.BlockSpec((2, nb, tn, w), lambda i: (0, 0, i, 0))

    def shp(width, dtype=F32):
        return jax.ShapeDtypeStruct((nb, n, width), dtype)

    shp2 = jax.ShapeDtypeStruct((2, nb, n, w), F32)
    s5_rows = tn // S5_CHUNK * nb
    return pl.pallas_call(
        _inproj_kernel,
        grid=(n // tn,),
        in_specs=[tok(D_MODEL), _resident((nb, N_MOD, D_MODEL)), _resident((1, D_MODEL)),
                  _resident((D_MODEL, IN_COLS)), _resident((128, 2 * RWKV_WIDTH)), _resident((1, 2 * RWKV_WIDTH)),
                  _resident((128, 2 * RWKV_WIDTH)), _resident((1, 2 * RWKV_WIDTH)), _resident((128, RWKV_WIDTH))],
        out_specs=[tok(S5_WIDTH), pl.BlockSpec((S5_GROUPS, s5_rows, lw5), lambda i: (0, i, 0)), tok(3 * w),
                   tok2, tok2, tok(w), tok(D_MODEL), tok(D_MODEL)],
        out_shape=[shp(S5_WIDTH), jax.ShapeDtypeStruct((S5_GROUPS, n // S5_CHUNK * nb, lw5), BF16), shp(3 * w),
                   shp2, shp2, shp(w), shp(D_MODEL, BF16), shp(D_MODEL, BF16)],
        scratch_shapes=[pltpu.VMEM((S5_WIDTH // LANE, nb * tn, LANE), F32)],
        compiler_params=_cparams("parallel"),
        name="inproj",
    )(x, mods, g, w_in, w2c, w0c, a2c, a0c, g2)


def _prep_kernel(up_ref, mid_ref, dn_ref, ic_ref, cw_ref, kk_w_ref, ka_ref, rk_ref, seg_ref,
                 r_ref, v_ref, kk_ref, k2_ref, bonus_ref, *, width, vertical, tiles_per_image):
    tm = mid_ref.shape[0]
    cw = cw_ref[...]
    mid = mid_ref[...]
    xpos = lax.broadcasted_iota(jnp.int32, (tm, 1), 0) % width
    not_first = xpos != 0
    not_last = xpos != width - 1

    rows = [(mid, 1)]
    if vertical:
        t = pl.program_id(0) % tiles_per_image
        up = jnp.where(t == 0, 0.0, up_ref[...])
        dn = jnp.where(t == tiles_per_image - 1, 0.0, dn_ref[...])
        ext = jnp.concatenate([up, mid, dn], axis=0)
        rows += [(ext[0:tm], 0), (ext[2 * width:2 * width + tm], 2)]

    def column(dx):
        return sum(base * cw[3 * dy + dx:3 * dy + dx + 1] for base, dy in rows)

    acc = (column(1) + jnp.where(not_first, pltpu.roll(column(0), 1, 0), 0.0)
           + jnp.where(not_last, pltpu.roll(column(2), tm - 1, 0), 0.0))

    w = RWKV_WIDTH
    r, k, v = acc[:, :w], acc[:, w:2 * w], acc[:, 2 * w:]
    seg = seg_ref[...]
    kk = k * kk_w_ref[...]
    kk = kk * lax.rsqrt(_dot_right01(kk * kk, seg) + 1e-12)
    ka = ka_ref[...]
    rrk = r * rk_ref[...]
    kf = k * (1.0 + (ic_ref[0] - 1.0) * ka)
    kb = k * (1.0 + (ic_ref[1] - 1.0) * ka)
    r_ref[...] = r
    v_ref[...] = v
    kk_ref[...] = kk
    k2_ref[0] = kf
    k2_ref[1] = kb
    bonus_ref[...] = _dot_right01(rrk * kf + rrk * kb, seg) * v


def _prep_call(rkv, iclr, cw, kk_w, ka, rk, seg, *, tokens_per_batch, width, vertical):
    n = rkv.shape[0]
    tm = min(SMALL_TILE, tokens_per_batch)
    rows_per_tile = tm // width
    n_rows = n // width
    w3 = 3 * RWKV_WIDTH
    w = RWKV_WIDTH
    tok = pl.BlockSpec((tm, w), lambda i: (i, 0))
    return pl.pallas_call(
        functools.partial(_prep_kernel, width=width, vertical=vertical, tiles_per_image=tokens_per_batch // tm),
        grid=(n // tm,),
        in_specs=[pl.BlockSpec((width, w3), lambda i: (jnp.maximum(i * rows_per_tile - 1, 0), 0)),
                  pl.BlockSpec((tm, w3), lambda i: (i, 0)),
                  pl.BlockSpec((width, w3), lambda i: (jnp.minimum((i + 1) * rows_per_tile, n_rows - 1), 0)),
                  pl.BlockSpec((2, tm, w), lambda i: (0, i, 0)),
                  _resident((9, w3)), _resident((1, w)), _resident((1, w)), _resident((1, w)), _resident((w, w))],
        out_specs=[tok, tok, tok, pl.BlockSpec((2, tm, w), lambda i: (0, i, 0)), tok],
        out_shape=[jax.ShapeDtypeStruct((n, w), F32)] * 3 + [jax.ShapeDtypeStruct((2, n, w), F32),
                                                              jax.ShapeDtypeStruct((n, w), F32)],
        compiler_params=_cparams("parallel"),
        name="prep",
    )(rkv, rkv, rkv, iclr, cw, kk_w, ka, rk, seg)


def _s5_kernel(uc_ref, ux_ref, wcat_ref, tfb_ref, cp_ref, ab_ref, y_ref, x_scr, h_scr, *, ctx_chunks, x_chunks, nb):
    ux = ux_ref[0]
    r0 = ctx_chunks * nb
    x_scr[:r0] = _dot(uc_ref[0], wcat_ref[0])
    x_scr[r0:] = _dot(ux, wcat_ref[0])
    ab = ab_ref[0]
    total = ctx_chunks + x_chunks

    def scan(carry, first, count, step, a1, a2, col):
        def body(i, hc):
            h, hs = hc
            off = pl.multiple_of((first + i * step) * nb, nb)
            h_scr[pl.ds(off, nb), col * 128:(col + 1) * 128] = h
            x = x_scr[pl.ds(off, nb), 2 * col * 128:(2 * col + 1) * 128]
            xs = x_scr[pl.ds(off, nb), (2 * col + 1) * 128:(2 * col + 2) * 128]
            return a1 * h + a2 * hs + x, a1 * hs - a2 * h + xs
        return lax.fori_loop(0, count, body, carry)

    zero = jnp.zeros((nb, 128), F32)
    scan((zero, zero), 0, total, 1, ab[0], ab[1], 0)
    carry = scan((zero, zero), ctx_chunks - 1, ctx_chunks, -1, ab[2], ab[3], 1)
    scan(carry, total - 1, x_chunks, -1, ab[2], ab[3], 1)

    y_ref[0] = _dot(ux, tfb_ref[0]) + _dot(h_scr[r0:, :].astype(BF16), cp_ref[0])


def _s5_call(u_c, u_x, wcat, tfb, cp, ab, *, nb):
    g = u_x.shape[0]
    ctx_chunks = u_c.shape[1] // nb
    x_chunks = u_x.shape[1] // nb
    rows = (ctx_chunks + x_chunks) * nb
    lw = S5_CHUNK * S5_GROUP
    return pl.pallas_call(
        functools.partial(_s5_kernel, ctx_chunks=ctx_chunks, x_chunks=x_chunks, nb=nb),
        grid=(g,),
        in_specs=[pl.BlockSpec((1, ctx_chunks * nb, lw), lambda i: (i, 0, 0)),
                  pl.BlockSpec((1, x_chunks * nb, lw), lambda i: (i, 0, 0)),
                  pl.BlockSpec((1, lw, 512), lambda i: (i, 0, 0)),
                  pl.BlockSpec((1, lw, lw), lambda i: (i, 0, 0)),
                  pl.BlockSpec((1, 256, lw), lambda i: (i, 0, 0)),
                  pl.BlockSpec((1, 4, nb, 128), lambda i: (i, 0, 0, 0))],
        out_specs=pl.BlockSpec((1, x_chunks * nb, lw), lambda i: (i, 0, 0)),
        out_shape=jax.ShapeDtypeStruct((g, x_chunks * nb, lw), F32),
        scratch_shapes=[pltpu.VMEM((rows, 512), F32), pltpu.VMEM((rows, 256), F32)],
        compiler_params=_cparams("parallel"),
        name="s5",
    )(u_c, u_x, wcat, tfb, cp, ab)


def _s5_matrices(lam_re, lam_im, log_dt, b_re, b_im, c_re, c_im, nb):
    L, P, GS = S5_CHUNK, S5_STATE, S5_GROUP
    hp = lax.Precision.HIGHEST
    dt = jnp.exp(log_dt)[..., None]
    ab_re = jnp.exp(dt * lam_re) * jnp.cos(dt * lam_im)
    ab_im = jnp.exp(dt * lam_re) * jnp.sin(dt * lam_im)
    den = lam_re * lam_re + lam_im * lam_im
    z_re = ((ab_re - 1.0) * lam_re + ab_im * lam_im) / den
    z_im = (ab_im * lam_re - (ab_re - 1.0) * lam_im) / den
    bb_re = z_re[..., None] * b_re - z_im[..., None] * b_im
    bb_im = z_re[..., None] * b_im + z_im[..., None] * b_re
    k = jnp.arange(L + 1, dtype=F32)[:, None, None, None]
    pw_re = jnp.exp(k * dt * lam_re) * jnp.cos(k * dt * lam_im)
    pw_im = jnp.exp(k * dt * lam_re) * jnp.sin(k * dt * lam_im)
    ca_re = c_re[None] * pw_re[:, :, :, None, :] - c_im[None] * pw_im[:, :, :, None, :]
    ca_im = c_re[None] * pw_im[:, :, :, None, :] + c_im[None] * pw_re[:, :, :, None, :]
    ab_re_k = pw_re[..., None] * bb_re[None] - pw_im[..., None] * bb_im[None]
    ab_im_k = pw_re[..., None] * bb_im[None] + pw_im[..., None] * bb_re[None]
    kern = (jnp.einsum('kdgip,dgpj->dgkij', ca_re[:L], bb_re, precision=hp)
            - jnp.einsum('kdgip,dgpj->dgkij', ca_im[:L], bb_im, precision=hp))

    s = jnp.arange(L)[:, None]
    t = jnp.arange(L)[None, :]

    def toeplitz(kd, lag, valid):
        pick = ((lag[:, :, None] == jnp.arange(L)) & valid[:, :, None]).astype(F32)
        return jnp.einsum('stk,gkij->gsjti', pick, kd, precision=hp).reshape(-1, L * GS, L * GS)

    tfb = toeplitz(kern[0], t - s, t >= s) + toeplitz(kern[1], s - t, s >= t)

    def state_in(d, powers):
        wr = ab_re_k[powers, d].transpose(1, 0, 3, 2).reshape(-1, L * GS, P)
        wi = ab_im_k[powers, d].transpose(1, 0, 3, 2).reshape(-1, L * GS, P)
        return jnp.concatenate([wr, wi, wi, wr], axis=-1)

    wcat = jnp.concatenate([state_in(0, L - 1 - jnp.arange(L)), state_in(1, jnp.arange(L))], axis=-1)

    def state_out(d, powers):
        cr = ca_re[powers, d].transpose(1, 3, 0, 2).reshape(-1, P, L * GS)
        ci = ca_im[powers, d].transpose(1, 3, 0, 2).reshape(-1, P, L * GS)
        return jnp.concatenate([cr, -ci], axis=1)

    cp = jnp.concatenate([state_out(0, 1 + jnp.arange(L)), state_out(1, L - jnp.arange(L))], axis=1)

    def step_coeffs(d):
        a1 = jnp.concatenate([pw_re[L, d], pw_re[L, d]], axis=-1)
        a2 = jnp.concatenate([-pw_im[L, d], pw_im[L, d]], axis=-1)
        return [a1, a2]

    ab = jnp.stack(step_coeffs(0) + step_coeffs(1), axis=1)
    ab = jnp.broadcast_to(ab[:, :, None, :], ab.shape[:2] + (nb, 128))
    return wcat.astype(BF16), tfb.astype(BF16), cp.astype(BF16), ab


def _rwkv_kernel(*refs, nc):
    pre_f32, pre_bf16, pre_w = refs[-3:]
    s_scr = refs[-4]
    i = pl.program_id(0)

    @pl.when(i == 0)
    def _():
        pre_f32[1] = jnp.zeros(pre_f32.shape[1:], F32)
        pre_bf16[1] = jnp.zeros(pre_bf16.shape[1:], BF16)
        pre_w[1] = jnp.zeros(pre_w.shape[1:], F32)
        s_scr[...] = jnp.zeros(s_scr.shape, F32)

    for parity in range(2):
        @pl.when(i % 2 == parity)
        def _():
            _rwkv_step(*refs, nc=nc, slot_new=parity)


def _rwkv_step(rf_ref, vf_ref, kkf_ref, rb_ref, vb_ref, kkb_ref, kf_ref, kb_ref, lwf_ref, lwb_ref,
               icf_ref, icb_ref, s0_ref, yf_ref, yb_ref, sT_ref, s_scr, pre_f32, pre_bf16, pre_w, *, nc, slot_new):
    slot_cur = 1 - slot_new
    c = jnp.maximum(pl.program_id(0) - 1, 0) % nc
    L, N = RWKV_CHUNK, RWKV_HEAD
    P = 2 * N
    n_pairs = RWKV_WIDTH // P

    row = lax.broadcasted_iota(jnp.int32, (L, P), 0)
    lane = lax.broadcasted_iota(jnp.int32, (L, P), 1)
    col = lane % N
    first = lane < N
    eye = (col == row).astype(F32)
    strict = (col < row, col > row)
    incl = (col <= row, col >= row)
    row1 = lax.broadcasted_iota(jnp.int32, (L, L), 0)
    col1 = lax.broadcasted_iota(jnp.int32, (L, L), 1)
    tri = (col1 <= row1, col1 >= row1)

    def bd(y):
        return jnp.concatenate([jnp.where(first, y, 0).astype(BF16), jnp.where(first, 0, y).astype(BF16)], axis=0)

    def pmm(x, y):
        return _dot(x.astype(BF16), bd(y))

    def pmm_tn(x, y):
        full = _dot(x.T.astype(BF16), y.astype(BF16))
        return jnp.where(first, full[:N], full[N:])

    bt = rf_ref.shape[0]
    per_dir = ((rf_ref, vf_ref, kkf_ref, kf_ref, lwf_ref, icf_ref), (rb_ref, vb_ref, kkb_ref, kb_ref, lwb_ref, icb_ref))
    for bl in range(bt):
        for d, (r_ref, v_ref, kk_ref, k_ref, lw_ref, ic_ref) in enumerate(per_dir):
            lw = lw_ref[bl]
            cum = _dot_left01(jnp.where(tri[d], 1.0, 0.0).astype(BF16), lw)
            tot = jnp.sum(lw, axis=0, keepdims=True)
            kk = kk_ref[bl]
            b = kk * ic_ref[bl]
            k = k_ref[bl]
            e_ninc = jnp.exp(-cum)
            e_rem = jnp.exp(tot - cum)
            pre_f32[slot_new, bl, d, 0] = -kk * jnp.exp(cum - lw)
            pre_f32[slot_new, bl, d, 1] = r_ref[bl] * jnp.exp(cum)
            pre_f32[slot_new, bl, d, 2] = v_ref[bl]
            pre_bf16[slot_new, bl, d, 0] = (b * e_ninc).astype(BF16)
            pre_bf16[slot_new, bl, d, 1] = (k * e_ninc).astype(BF16)
            pre_bf16[slot_new, bl, d, 2] = (b * e_rem).astype(BF16)
            pre_bf16[slot_new, bl, d, 3] = (k * e_rem).astype(BF16)
            pre_w[slot_new, bl, d] = jnp.exp(tot)

    chains = [(bl, d, p) for bl in range(bt) for d in range(2) for p in range(n_pairs)]

    def pair(scr, j):
        return [scr[slot_cur, bl, d, j, :, p * P:(p + 1) * P] for bl, d, p in chains]

    a_tp, r_tp, vp = (pair(pre_f32, j) for j in range(3))
    b_tp, k_tp, b_hp, k_hp = (pair(pre_bf16, j) for j in range(4))
    w_p = [pre_w[slot_cur, bl, d, :, p * P:(p + 1) * P] for bl, d, p in chains]
    def rows2(x, y):
        return jnp.concatenate([x, y], axis=0)

    def cols2(x, y):
        return jnp.concatenate([x, y], axis=1)

    ar = [rows2(a, r).astype(BF16) for a, r in zip(a_tp, r_tp)]
    gram = [_dot_nt(x, rows2(bd(y), bd(z))) for x, y, z in zip(ar, b_tp, k_tp)]
    n_ab = [jnp.where(strict[d], g[:L, :P], 0.0) for (_, d, _), g in zip(chains, gram)]
    a_ak = [jnp.where(strict[d], g[:L, P:], 0.0) for (_, d, _), g in zip(chains, gram)]
    a_rb = [jnp.where(incl[d], g[L:, :P], 0.0).astype(BF16) for (_, d, _), g in zip(chains, gram)]
    a_rk = [jnp.where(incl[d], g[L:, P:], 0.0) for (_, d, _), g in zip(chains, gram)]
    kv = [pmm(rows2(x, y), z) for x, y, z in zip(a_ak, a_rk, vp)]
    inv = [eye + x for x in n_ab]
    pw = [pmm(x, x) for x in n_ab]
    for _ in range(4):
        both = [pmm(rows2(x, t), x) for x, t in zip(pw, inv)]
        inv = [t + z[L:] for t, z in zip(inv, both)]
        pw = [z[:L] for z in both]
    inv = [t + pmm(t, x) for t, x in zip(inv, pw)]
    invb = [x.astype(BF16) for x in inv]
    apvp = [_dot(t, cols2(bd(a), bd(z[:L]))) for t, a, z in zip(invb, a_tp, kv)]
    a_p = [z[:, :P] for z in apvp]
    v_p = [z[:, P:] for z in apvp]
    rb = [_dot(x, cols2(bd(y), bd(z))) for x, y, z in zip(a_rb, a_p, v_p)]
    r_p = [r + z[:, :P] for r, z in zip(r_tp, rb)]
    y_loc = [z[:, P:] + w[L:] for z, w in zip(rb, kv)]
    tn = [_dot(z.T.astype(BF16), y) for z, y in zip(apvp, b_hp)]
    p_lr = [jnp.where(first, z[:N], z[N:P]) for z in tn]
    q = [jnp.where(first, z[P:P + N], z[P + N:]) + pmm_tn(x, y) for z, x, y in zip(tn, vp, k_hp)]

    s = [jnp.where(c == 0, s0_ref[d, bl, p], s_scr[bl, d, p]) for bl, d, p in chains]
    s_bd = [bd(x) for x in s]
    y = [_dot_nt(x.astype(BF16), z) + yl for x, z, yl in zip(r_p, s_bd, y_loc)]
    s_new = [x * w + pmm(x, pl_) + qq for x, w, pl_, qq in zip(s, w_p, p_lr, q)]
    for (bl, d, p), yy, ss in zip(chains, y, s_new):
        (yf_ref, yb_ref)[d][bl, :, p * P:(p + 1) * P] = yy
        s_scr[bl, d, p] = ss
        sT_ref[d, bl, p] = ss


def _rwkv_call(r, v, kk, k2, lw2, ic2, s0):
    nb, n, w = r.shape
    bt = RWKV_BATCH
    nc = n // RWKV_CHUNK
    items = nb // bt * nc

    def fwd_chunk(item):
        return item % nc

    def bwd_chunk(item):
        return nc - 1 - item % nc

    def item_in(i):
        return jnp.minimum(i, items - 1)

    def item_out(i):
        return jnp.maximum(i - 1, 0)

    def tok(chunk, item):
        return pl.BlockSpec((bt, RWKV_CHUNK, w), lambda i: (item(i) // nc, chunk(item(i)), 0))

    both = [pl.BlockSpec((None, bt, RWKV_CHUNK, w), lambda i: (0, item_in(i) // nc, fwd_chunk(item_in(i)), 0)),
            pl.BlockSpec((None, bt, RWKV_CHUNK, w), lambda i: (1, item_in(i) // nc, bwd_chunk(item_in(i)), 0))]
    st = pl.BlockSpec((2, bt) + RWKV_STATE, lambda i: (0, item_out(i) // nc, 0, 0, 0))
    fwd_in, bwd_in = tok(fwd_chunk, item_in), tok(bwd_chunk, item_in)
    out = jax.ShapeDtypeStruct((nb, n, w), F32)
    return pl.pallas_call(
        functools.partial(_rwkv_kernel, nc=nc),
        grid=(items + 1,),
        in_specs=[fwd_in, fwd_in, fwd_in, bwd_in, bwd_in, bwd_in, *both, *both, *both, st],
        out_specs=[tok(fwd_chunk, item_out), tok(bwd_chunk, item_out), st],
        out_shape=[out, out, jax.ShapeDtypeStruct((2, nb) + RWKV_STATE, F32)],
        scratch_shapes=[pltpu.VMEM((bt, 2) + RWKV_STATE, F32),
                        pltpu.VMEM((2, bt, 2, 3, RWKV_CHUNK, w), F32),
                        pltpu.VMEM((2, bt, 2, 4, RWKV_CHUNK, w), BF16),
                        pltpu.VMEM((2, bt, 2, 1, w), F32)],
        compiler_params=_cparams("arbitrary"),
        name="rwkv",
    )(r, v, kk, r, v, kk, k2, k2, lw2, lw2, ic2, ic2, s0)


def _out_kernel(x_ref, mod_ref, ya_ref, u_ref, ybf_ref, ybb_ref, bonus_ref, gg_ref, sga_ref, sgb_ref,
                s5d_ref, wglu_ref, wproj_ref, lng_ref, lnb_ref, seg_ref, wo_ref, wout_ref, o_ref, ya_scr):
    nb, tn, _ = x_ref.shape
    rows = nb * tn

    def get(ref):
        return ref[...].reshape(rows, ref.shape[-1]).astype(F32)

    per_tile = LANE // S5_GROUP
    for cl in range(tn // S5_CHUNK):
        for q in range(S5_WIDTH // LANE):
            per_g = [ya_ref[q * per_tile + gq, cl * nb:(cl + 1) * nb, :] for gq in range(per_tile)]
            for t in range(S5_CHUNK):
                sl = slice(t * S5_GROUP, (t + 1) * S5_GROUP)
                ya_scr[q, pl.ds(cl * S5_CHUNK + t, nb, stride=tn), :] = jnp.concatenate(
                    [y[:, sl] for y in per_g], axis=1)
    ya_tok = jnp.concatenate([ya_scr[q] for q in range(S5_WIDTH // LANE)], axis=1)

    ya = jax.nn.gelu(ya_tok + s5d_ref[...] * get(u_ref))
    ya = ya * jax.nn.sigmoid(_dot(ya.astype(BF16), wglu_ref[...]))
    pa = _dot(ya.astype(BF16), wproj_ref[...])

    seg = seg_ref[...]
    yb = get(ybf_ref) + get(ybb_ref)
    mu = _dot_right01(yb, seg) * (1.0 / RWKV_HEAD)
    cen = yb - mu
    var = _dot_right01(cen * cen, seg) * (1.0 / RWKV_HEAD)
    yb = cen * lax.rsqrt(var + LN_X_EPS) * lng_ref[...] + lnb_ref[...] + get(bonus_ref)
    pb = _dot((yb * get(gg_ref)).astype(BF16), wo_ref[...])

    merged = get(sga_ref) * pa + get(sgb_ref) * pb
    out = _dot(merged.astype(BF16), wout_ref[...]).reshape(nb, tn, D_MODEL)
    o_ref[...] = x_ref[...] + mod_ref[:, 5:6] * out


def _out_call(x, mods, ya, u, ybf, ybb, bonus, gg, sga, sgb, s5d, wglu, wproj, lng, lnb, seg, wo, wout):
    nb, n, _ = x.shape
    tn = BATCH_TILE
    w = RWKV_WIDTH

    def tok(width):
        return pl.BlockSpec((nb, tn, width), lambda i: (0, i, 0))

    s5_rows = tn // S5_CHUNK * nb
    return pl.pallas_call(
        _out_kernel,
        grid=(n // tn,),
        in_specs=[tok(D_MODEL), _resident((nb, N_MOD, D_MODEL)),
                  pl.BlockSpec((S5_GROUPS, s5_rows, S5_CHUNK * S5_GROUP), lambda i: (0, i, 0)), tok(w),
                  tok(w), tok(w), tok(w), tok(w), tok(D_MODEL), tok(D_MODEL),
                  _resident((1, w)), _resident((w, w)), _resident((w, D_MODEL)), _resident((1, w)),
                  _resident((1, w)), _resident((w, w)), _resident((w, D_MODEL)), _resident((D_MODEL, D_MODEL))],
        out_specs=tok(D_MODEL),
        out_shape=jax.ShapeDtypeStruct((nb, n, D_MODEL), F32),
        scratch_shapes=[pltpu.VMEM((S5_WIDTH // LANE, nb * tn, LANE), F32)],
        compiler_params=_cparams("parallel"),
        name="outk",
    )(x, mods, ya, u, ybf, ybb, bonus, gg, sga, sgb, s5d, wglu, wproj, lng, lnb, seg, wo, wout)


def _block_diag2(a):
    z = jnp.zeros_like(a[0])
    return jnp.concatenate([jnp.concatenate([a[0], z], axis=1), jnp.concatenate([z, a[1]], axis=1)], axis=0)


def kernel(x, c, ctx, c_ctx, w_mod, b_mod, norm_g, ffn_w_gate, ffn_w_up, ffn_w_down, w_in, s5_A_re, s5_A_im, s5_log_dt, s5_B_re, s5_B_im, s5_C_re, s5_C_im, s5_D, s5_w_glu, s5_w_proj, rwkv_conv, rwkv_w0, rwkv_w2, rwkv_a0, rwkv_a2, rwkv_g2, rwkv_k_k, rwkv_k_a, rwkv_r_k, rwkv_ln_g, rwkv_ln_b, rwkv_w_o, w_out, final_g):
    nb, seq, d = x.shape
    n_ctx = ctx.shape[1]
    l = 0
    w = RWKV_WIDTH

    mod_rows = 16
    cc = jnp.concatenate([c, c_ctx[None], jnp.zeros((mod_rows - nb - 1, d), F32)], axis=0)
    mods = _mod_call(cc, w_mod[l], b_mod[l][None]).reshape(mod_rows, N_MOD, d)

    xt = x.reshape(nb * seq, d)
    ct = ctx.reshape(nb * n_ctx, d)
    row = lambda a: a.reshape(1, -1)
    bf = lambda a: a.astype(BF16)

    f1 = (row(norm_g[l, 0]), bf(ffn_w_gate[l, 0]), bf(ffn_w_up[l, 0]), bf(ffn_w_down[l, 0]), row(final_g))
    x1 = _ffn_call(xt, mods, *f1, j=0, tokens_per_batch=seq, shared_row=None, final_norm=False)
    c1 = _ffn_call(ct, mods, *f1, j=0, tokens_per_batch=n_ctx, shared_row=nb, final_norm=False)

    ip = (row(norm_g[l, 1]), bf(w_in[l]), bf(_block_diag2(rwkv_w2[l])), row(rwkv_w0[l]),
          bf(_block_diag2(rwkv_a2[l])), row(rwkv_a0[l]), bf(rwkv_g2[l]))
    mods_x = mods[:nb]
    mods_c = jnp.broadcast_to(mods[nb:nb + 1], (nb, N_MOD, d))
    u_x, us5_x, rkv_x, lw_x, ic_x, gg_x, sga_x, sgb_x = _inproj_call(x1.reshape(nb, seq, d), mods_x, *ip)
    _, us5_c, rkv_c, lw_c, ic_c, _, _, _ = _inproj_call(c1.reshape(nb, n_ctx, d), mods_c, *ip)

    def flat(a):
        return a.reshape(a.shape[:-3] + (a.shape[-3] * a.shape[-2], a.shape[-1]))

    s5p = _s5_matrices(s5_A_re[l], s5_A_im[l], s5_log_dt[l], s5_B_re[l], s5_B_im[l], s5_C_re[l], s5_C_im[l], nb)
    ya_s5 = _s5_call(us5_c, us5_x, *s5p, nb=nb)

    seg = (jnp.arange(w)[:, None] // RWKV_HEAD == jnp.arange(w)[None, :] // RWKV_HEAD).astype(BF16)
    pp = (rwkv_conv[l].reshape(9, 3 * w), row(rwkv_k_k[l]), row(rwkv_k_a[l]), row(rwkv_r_k[l]), seg)
    r_c, v_c, kk_c, k2_c, _ = _prep_call(flat(rkv_c), flat(ic_c), *pp, tokens_per_batch=n_ctx, width=n_ctx,
                                         vertical=False)
    r_x, v_x, kk_x, k2_x, bonus = _prep_call(flat(rkv_x), flat(ic_x), *pp, tokens_per_batch=seq, width=GRID_W,
                                             vertical=True)

    def by_batch(a, n):
        return a.reshape(a.shape[:-2] + (nb, n, a.shape[-1]))

    s0 = jnp.zeros((2, nb) + RWKV_STATE, F32)
    _, _, s_ctx = _rwkv_call(*(by_batch(a, n_ctx) for a in (r_c, v_c, kk_c, k2_c)), lw_c, ic_c, s0)
    ybf, ybb, _ = _rwkv_call(*(by_batch(a, seq) for a in (r_x, v_x, kk_x, k2_x)), lw_x, ic_x, s_ctx)

    x2 = _out_call(by_batch(x1, seq), mods_x, ya_s5, u_x, ybf, ybb, by_batch(bonus, seq), gg_x, sga_x,
                   sgb_x, row(s5_D[l]), bf(s5_w_glu[l]), bf(s5_w_proj[l]), row(rwkv_ln_g[l]), row(rwkv_ln_b[l]),
                   seg, bf(rwkv_w_o[l]), bf(w_out[l]))

    f2 = (row(norm_g[l, 2]), bf(ffn_w_gate[l, 1]), bf(ffn_w_up[l, 1]), bf(ffn_w_down[l, 1]), row(final_g))
    out = _ffn_call(flat(x2), mods, *f2, j=2, tokens_per_batch=seq, shared_row=None, final_norm=True)
    return out.reshape(nb, seq, d)
```

```python
import functools
import math

import jax
import jax.numpy as jnp
from jax import lax
from jax.experimental import pallas as pl
from jax.experimental.pallas import tpu as pltpu

F32 = jnp.float32
BF16 = jnp.bfloat16

D_MODEL = 1024
N_MOD = 9
FFN_DIM = 2816
RMS_EPS = 1e-6
GRID_W = 64
S5_WIDTH = 512
S5_GROUP = 16
S5_GROUPS = 32
S5_STATE = 64
RWKV_WIDTH = 512
RWKV_HEAD = 64
RWKV_HEADS = 8
LN_X_EPS = 64e-5
LANE = 128

S5_CHUNK = 16
RWKV_CHUNK = 64
TOKEN_TILE = 512
SMALL_TILE = 256
BATCH_TILE = 64
S5_SCAN_UNROLL = 4
ROW_SPLIT = 2
RWKV_STATE = (RWKV_HEADS // 2, RWKV_HEAD, 2 * RWKV_HEAD)
RWKV_BATCH = 2
VMEM_LIMIT = 56 * 1024 * 1024


def _cparams(*sem):
    return pltpu.CompilerParams(dimension_semantics=sem, vmem_limit_bytes=VMEM_LIMIT)


def _resident(shape):
    nd = len(shape)
    return pl.BlockSpec(shape, lambda *_: (0,) * nd, pipeline_mode=pl.Buffered(1))


def _dot(a, b):
    return jnp.dot(a, b, preferred_element_type=F32)


def _dot_nt(a, b):
    return lax.dot_general(a, b, (((1,), (1,)), ((), ())), preferred_element_type=F32)


def _split2(x):
    hi = x.astype(BF16)
    return hi, (x - hi.astype(F32)).astype(BF16)


def _dot_left01(m01, x):
    hi, lo = _split2(x)
    return _dot(m01, hi) + _dot(m01, lo)


def _dot_right01(x, m01):
    hi, lo = _split2(x)
    return _dot(hi, m01) + _dot(lo, m01)


def _rms_mod(x, g, shift, scale):
    y = x * lax.rsqrt(jnp.mean(x * x, axis=-1, keepdims=True) + RMS_EPS)
    return (y * g) * (1.0 + scale) + shift


def _mod_kernel(c_ref, w_ref, b_ref, o_ref):
    c = c_ref[...]
    s = c * jax.nn.sigmoid(c)
    o_ref[...] = jnp.dot(s, w_ref[...], precision=lax.Precision.HIGHEST, preferred_element_type=F32) + b_ref[...]


def _mod_call(cc, w_mod, b_mod):
    rows = cc.shape[0]
    return pl.pallas_call(
        _mod_kernel,
        grid=(N_MOD,),
        in_specs=[pl.BlockSpec((rows, D_MODEL), lambda j: (0, 0)),
                  pl.BlockSpec((D_MODEL, D_MODEL), lambda j: (0, j)),
                  pl.BlockSpec((1, D_MODEL), lambda j: (0, j))],
        out_specs=pl.BlockSpec((rows, D_MODEL), lambda j: (0, j)),
        out_shape=jax.ShapeDtypeStruct((rows, N_MOD * D_MODEL), F32),
        compiler_params=_cparams("arbitrary"),
        name="mod",
    )(cc, w_mod, b_mod)


def _ffn_kernel(x_ref, mod_ref, g_ref, wg_ref, wu_ref, wd_ref, fg_ref, o_ref, *, j, final_norm):
    x = x_ref[...]
    m = mod_ref[0]
    h = _rms_mod(x, g_ref[...], m[3 * j:3 * j + 1], m[3 * j + 1:3 * j + 2]).astype(BF16)
    a = _dot(h, wg_ref[...])
    a = (a * jax.nn.sigmoid(a)) * _dot(h, wu_ref[...])
    y = _dot(a.astype(BF16), wd_ref[...])
    out = x + 0.5 * m[3 * j + 2:3 * j + 3] * y
    if final_norm:
        out = out * lax.rsqrt(jnp.mean(out * out, axis=-1, keepdims=True) + RMS_EPS) * fg_ref[...]
    o_ref[...] = out


def _mod_spec(n_tiles, tiles_per_batch, shared_row):
    if shared_row is not None:
        return pl.BlockSpec((1, N_MOD, D_MODEL), lambda i: (shared_row, 0, 0))
    assert n_tiles % tiles_per_batch == 0
    return pl.BlockSpec((1, N_MOD, D_MODEL), lambda i: (i // tiles_per_batch, 0, 0))


def _ffn_call(x, mods, g, wg, wu, wd, fg, *, j, tokens_per_batch, shared_row, final_norm):
    n = x.shape[0]
    tm = min(TOKEN_TILE, tokens_per_batch)
    tok = pl.BlockSpec((tm, D_MODEL), lambda i: (i, 0))
    return pl.pallas_call(
        functools.partial(_ffn_kernel, j=j, final_norm=final_norm),
        grid=(n // tm,),
        in_specs=[tok, _mod_spec(n // tm, tokens_per_batch // tm, shared_row), _resident((1, D_MODEL)),
                  _resident((D_MODEL, FFN_DIM)), _resident((D_MODEL, FFN_DIM)), _resident((FFN_DIM, D_MODEL)),
                  _resident((1, D_MODEL))],
        out_specs=tok,
        out_shape=jax.ShapeDtypeStruct((n, D_MODEL), F32),
        compiler_params=_cparams("parallel"),
        name="ffn",
    )(x, mods, g, wg, wu, wd, fg)


O_U = S5_WIDTH
O_RKV = O_U + 3 * RWKV_WIDTH
O_WD = O_RKV + 128
O_AD = O_WD + 128
O_GD = O_AD + 128
O_GA = O_GD + D_MODEL
IN_COLS = O_GA + D_MODEL
DECAY_SCALE = math.exp(-0.5)


def _inproj_kernel(x_ref, mod_ref, g_ref, w_ref, w2_ref, w0_ref, a2_ref, a0_ref, g2_ref,
                   u_ref, us5_ref, rkv_ref, lw_ref, ic_ref, gg_ref, sga_ref, sgb_ref, u_scr):
    nb, tn, _ = x_ref.shape
    hb = nb // ROW_SPLIT
    for part in range(ROW_SPLIT):
        bs = slice(part * hb, (part + 1) * hb)
        rows = hb * tn
        m = mod_ref[bs]
        h = _rms_mod(x_ref[bs], g_ref[...], m[:, 3:4], m[:, 4:5]).reshape(rows, D_MODEL).astype(BF16)
        proj = _dot(h, w_ref[...])

        def put(ref, val):
            ref[bs] = val.reshape(hb, tn, val.shape[-1]).astype(ref.dtype)

        u = proj[:, :O_U]
        put(u_ref, u)
        put(rkv_ref, proj[:, O_U:O_RKV])
        w_log = w0_ref[...] + _dot(jnp.tanh(proj[:, O_RKV:O_WD]).astype(BF16), w2_ref[...])
        lw = -DECAY_SCALE * jax.nn.sigmoid(w_log)
        ic = jax.nn.sigmoid(a0_ref[...] + _dot(proj[:, O_WD:O_AD].astype(BF16), a2_ref[...]))
        for d in range(2):
            lw_ref[d, bs] = lw[:, d * RWKV_WIDTH:(d + 1) * RWKV_WIDTH].reshape(hb, tn, RWKV_WIDTH)
            ic_ref[d, bs] = ic[:, d * RWKV_WIDTH:(d + 1) * RWKV_WIDTH].reshape(hb, tn, RWKV_WIDTH)
        put(gg_ref, _dot(jax.nn.sigmoid(proj[:, O_AD:O_GD]).astype(BF16), g2_ref[...]))
        put(sga_ref, jax.nn.sigmoid(proj[:, O_GD:O_GA]))
        put(sgb_ref, jax.nn.sigmoid(proj[:, O_GA:]))
        for q in range(S5_WIDTH // LANE):
            u_scr[q, part * rows:(part + 1) * rows] = u[:, q * LANE:(q + 1) * LANE]

    per_tile = LANE // S5_GROUP
    for q in range(S5_WIDTH // LANE):
        taps = [[u_scr[q, pl.ds(cl * S5_CHUNK + t, nb, stride=tn), :] for t in range(S5_CHUNK)]
                for cl in range(tn // S5_CHUNK)]
        for gq in range(per_tile):
            sl = slice(gq * S5_GROUP, (gq + 1) * S5_GROUP)
            rows_g = [jnp.concatenate([a[:, sl] for a in chunk_taps], axis=1) for chunk_taps in taps]
            us5_ref[q * per_tile + gq] = jnp.concatenate(rows_g, axis=0).astype(BF16)


def _inproj_call(x, mods, g, w_in, w2c, w0c, a2c, a0c, g2):
    nb, n, _ = x.shape
    tn = BATCH_TILE
    w = RWKV_WIDTH
    lw5 = S5_CHUNK * S5_GROUP

    def tok(width):
        return pl.BlockSpec((nb, tn, width), lambda i: (0, i, 0))

    tok2 = pl.BlockSpec((2, nb, tn, w), lambda i: (0, 0, i, 0))

    def shp(width, dtype=F32):
        return jax.ShapeDtypeStruct((nb, n, width), dtype)

    shp2 = jax.ShapeDtypeStruct((2, nb, n, w), F32)
    s5_rows = tn // S5_CHUNK * nb
    return pl.pallas_call(
        _inproj_kernel,
        grid=(n // tn,),
        in_specs=[tok(D_MODEL), _resident((nb, N_MOD, D_MODEL)), _resident((1, D_MODEL)),
                  _resident((D_MODEL, IN_COLS)), _resident((128, 2 * RWKV_WIDTH)), _resident((1, 2 * RWKV_WIDTH)),
                  _resident((128, 2 * RWKV_WIDTH)), _resident((1, 2 * RWKV_WIDTH)), _resident((128, RWKV_WIDTH))],
        out_specs=[tok(S5_WIDTH), pl.BlockSpec((S5_GROUPS, s5_rows, lw5), lambda i: (0, i, 0)), tok(3 * w),
                   tok2, tok2, tok(w), tok(D_MODEL), tok(D_MODEL)],
        out_shape=[shp(S5_WIDTH), jax.ShapeDtypeStruct((S5_GROUPS, n // S5_CHUNK * nb, lw5), BF16), shp(3 * w),
                   shp2, shp2, shp(w), shp(D_MODEL, BF16), shp(D_MODEL, BF16)],
        scratch_shapes=[pltpu.VMEM((S5_WIDTH // LANE, nb * tn, LANE), F32)],
        compiler_params=_cparams("parallel"),
        name="inproj",
    )(x, mods, g, w_in, w2c, w0c, a2c, a0c, g2)


def _prep_kernel(up_ref, mid_ref, dn_ref, ic_ref, cw_ref, kk_w_ref, ka_ref, rk_ref, seg_ref,
                 r_ref, v_ref, kk_ref, k2_ref, bonus_ref, *, width, vertical, tiles_per_image):
    tm = mid_ref.shape[0]
    cw = cw_ref[...]
    mid = mid_ref[...]
    xpos = lax.broadcasted_iota(jnp.int32, (tm, 1), 0) % width
    not_first = xpos != 0
    not_last = xpos != width - 1

    rows = [(mid, 1)]
    if vertical:
        t = pl.program_id(0) % tiles_per_image
        up = jnp.where(t == 0, 0.0, up_ref[...])
        dn = jnp.where(t == tiles_per_image - 1, 0.0, dn_ref[...])
        ext = jnp.concatenate([up, mid, dn], axis=0)
        rows += [(ext[0:tm], 0), (ext[2 * width:2 * width + tm], 2)]

    def column(dx):
        return sum(base * cw[3 * dy + dx:3 * dy + dx + 1] for base, dy in rows)

    acc = (column(1) + jnp.where(not_first, pltpu.roll(column(0), 1, 0), 0.0)
           + jnp.where(not_last, pltpu.roll(column(2), tm - 1, 0), 0.0))

    w = RWKV_WIDTH
    r, k, v = acc[:, :w], acc[:, w:2 * w], acc[:, 2 * w:]
    seg = seg_ref[...]
    kk = k * kk_w_ref[...]
    kk = kk * lax.rsqrt(_dot_right01(kk * kk, seg) + 1e-12)
    ka = ka_ref[...]
    rrk = r * rk_ref[...]
    kf = k * (1.0 + (ic_ref[0] - 1.0) * ka)
    kb = k * (1.0 + (ic_ref[1] - 1.0) * ka)
    r_ref[...] = r
    v_ref[...] = v
    kk_ref[...] = kk
    k2_ref[0] = kf
    k2_ref[1] = kb
    bonus_ref[...] = _dot_right01(rrk * kf + rrk * kb, seg) * v


def _prep_call(rkv, iclr, cw, kk_w, ka, rk, seg, *, tokens_per_batch, width, vertical):
    n = rkv.shape[0]
    tm = min(SMALL_TILE, tokens_per_batch)
    rows_per_tile = tm // width
    n_rows = n // width
    w3 = 3 * RWKV_WIDTH
    w = RWKV_WIDTH
    tok = pl.BlockSpec((tm, w), lambda i: (i, 0))
    return pl.pallas_call(
        functools.partial(_prep_kernel, width=width, vertical=vertical, tiles_per_image=tokens_per_batch // tm),
        grid=(n // tm,),
        in_specs=[pl.BlockSpec((width, w3), lambda i: (jnp.maximum(i * rows_per_tile - 1, 0), 0)),
                  pl.BlockSpec((tm, w3), lambda i: (i, 0)),
                  pl.BlockSpec((width, w3), lambda i: (jnp.minimum((i + 1) * rows_per_tile, n_rows - 1), 0)),
                  pl.BlockSpec((2, tm, w), lambda i: (0, i, 0)),
                  _resident((9, w3)), _resident((1, w)), _resident((1, w)), _resident((1, w)), _resident((w, w))],
        out_specs=[tok, tok, tok, pl.BlockSpec((2, tm, w), lambda i: (0, i, 0)), tok],
        out_shape=[jax.ShapeDtypeStruct((n, w), F32)] * 3 + [jax.ShapeDtypeStruct((2, n, w), F32),
                                                              jax.ShapeDtypeStruct((n, w), F32)],
        compiler_params=_cparams("parallel"),
        name="prep",
    )(rkv, rkv, rkv, iclr, cw, kk_w, ka, rk, seg)


def _s5_kernel(uc_ref, ux_ref, wcat_ref, tfb_ref, cp_ref, ab_ref, y_ref, x_scr, h_scr, *, ctx_chunks, x_chunks, nb):
    ux = ux_ref[0]
    r0 = ctx_chunks * nb
    x_scr[:r0] = _dot(uc_ref[0], wcat_ref[0])
    x_scr[r0:] = _dot(ux, wcat_ref[0])
    ab = ab_ref[0]
    total = ctx_chunks + x_chunks

    def body(k, carry):
        hf, hfs, hb, hbs = carry
        cb = jnp.where(k < ctx_chunks, ctx_chunks - 1 - k, total + ctx_chunks - 1 - k)
        off_f = pl.multiple_of(k * nb, nb)
        off_b = pl.multiple_of(cb * nb, nb)
        h_scr[pl.ds(off_f, nb), 0:LANE] = hf
        h_scr[pl.ds(off_b, nb), LANE:2 * LANE] = hb
        xf, xfs = x_scr[pl.ds(off_f, nb), 0:LANE], x_scr[pl.ds(off_f, nb), LANE:2 * LANE]
        xb, xbs = x_scr[pl.ds(off_b, nb), 2 * LANE:3 * LANE], x_scr[pl.ds(off_b, nb), 3 * LANE:4 * LANE]
        return (ab[0] * hf + ab[1] * hfs + xf, ab[0] * hfs - ab[1] * hf + xfs,
                ab[2] * hb + ab[3] * hbs + xb, ab[2] * hbs - ab[3] * hb + xbs)

    zero = jnp.zeros((nb, LANE), F32)
    lax.fori_loop(0, total, body, (zero,) * 4, unroll=S5_SCAN_UNROLL)

    y_ref[0] = _dot(ux, tfb_ref[0]) + _dot(h_scr[r0:, :].astype(BF16), cp_ref[0])


def _s5_call(u_c, u_x, wcat, tfb, cp, ab, *, nb):
    g = u_x.shape[0]
    ctx_chunks = u_c.shape[1] // nb
    x_chunks = u_x.shape[1] // nb
    rows = (ctx_chunks + x_chunks) * nb
    lw = S5_CHUNK * S5_GROUP
    return pl.pallas_call(
        functools.partial(_s5_kernel, ctx_chunks=ctx_chunks, x_chunks=x_chunks, nb=nb),
        grid=(g,),
        in_specs=[pl.BlockSpec((1, ctx_chunks * nb, lw), lambda i: (i, 0, 0)),
                  pl.BlockSpec((1, x_chunks * nb, lw), lambda i: (i, 0, 0)),
                  pl.BlockSpec((1, lw, 512), lambda i: (i, 0, 0)),
                  pl.BlockSpec((1, lw, lw), lambda i: (i, 0, 0)),
                  pl.BlockSpec((1, 256, lw), lambda i: (i, 0, 0)),
                  pl.BlockSpec((1, 4, nb, 128), lambda i: (i, 0, 0, 0))],
        out_specs=pl.BlockSpec((1, x_chunks * nb, lw), lambda i: (i, 0, 0)),
        out_shape=jax.ShapeDtypeStruct((g, x_chunks * nb, lw), F32),
        scratch_shapes=[pltpu.VMEM((rows, 512), F32), pltpu.VMEM((rows, 256), F32)],
        compiler_params=_cparams("parallel"),
        name="s5",
    )(u_c, u_x, wcat, tfb, cp, ab)


def _s5_matrices(lam_re, lam_im, log_dt, b_re, b_im, c_re, c_im, nb):
    L, P, GS = S5_CHUNK, S5_STATE, S5_GROUP
    hp = lax.Precision.HIGHEST
    dt = jnp.exp(log_dt)[..., None]
    ab_re = jnp.exp(dt * lam_re) * jnp.cos(dt * lam_im)
    ab_im = jnp.exp(dt * lam_re) * jnp.sin(dt * lam_im)
    den = lam_re * lam_re + lam_im * lam_im
    z_re = ((ab_re - 1.0) * lam_re + ab_im * lam_im) / den
    z_im = (ab_im * lam_re - (ab_re - 1.0) * lam_im) / den
    bb_re = z_re[..., None] * b_re - z_im[..., None] * b_im
    bb_im = z_re[..., None] * b_im + z_im[..., None] * b_re
    k = jnp.arange(L + 1, dtype=F32)[:, None, None, None]
    pw_re = jnp.exp(k * dt * lam_re) * jnp.cos(k * dt * lam_im)
    pw_im = jnp.exp(k * dt * lam_re) * jnp.sin(k * dt * lam_im)
    ca_re = c_re[None] * pw_re[:, :, :, None, :] - c_im[None] * pw_im[:, :, :, None, :]
    ca_im = c_re[None] * pw_im[:, :, :, None, :] + c_im[None] * pw_re[:, :, :, None, :]
    ab_re_k = pw_re[..., None] * bb_re[None] - pw_im[..., None] * bb_im[None]
    ab_im_k = pw_re[..., None] * bb_im[None] + pw_im[..., None] * bb_re[None]
    kern = (jnp.einsum('kdgip,dgpj->dgkij', ca_re[:L], bb_re, precision=hp)
            - jnp.einsum('kdgip,dgpj->dgkij', ca_im[:L], bb_im, precision=hp))

    s = jnp.arange(L)[:, None]
    t = jnp.arange(L)[None, :]

    def toeplitz(kd, lag, valid):
        pick = ((lag[:, :, None] == jnp.arange(L)) & valid[:, :, None]).astype(F32)
        return jnp.einsum('stk,gkij->gsjti', pick, kd, precision=hp).reshape(-1, L * GS, L * GS)

    tfb = toeplitz(kern[0], t - s, t >= s) + toeplitz(kern[1], s - t, s >= t)

    def state_in(d, powers):
        wr = ab_re_k[powers, d].transpose(1, 0, 3, 2).reshape(-1, L * GS, P)
        wi = ab_im_k[powers, d].transpose(1, 0, 3, 2).reshape(-1, L * GS, P)
        return jnp.concatenate([wr, wi, wi, wr], axis=-1)

    wcat = jnp.concatenate([state_in(0, L - 1 - jnp.arange(L)), state_in(1, jnp.arange(L))], axis=-1)

    def state_out(d, powers):
        cr = ca_re[powers, d].transpose(1, 3, 0, 2).reshape(-1, P, L * GS)
        ci = ca_im[powers, d].transpose(1, 3, 0, 2).reshape(-1, P, L * GS)
        return jnp.concatenate([cr, -ci], axis=1)

    cp = jnp.concatenate([state_out(0, 1 + jnp.arange(L)), state_out(1, L - jnp.arange(L))], axis=1)

    def step_coeffs(d):
        a1 = jnp.concatenate([pw_re[L, d], pw_re[L, d]], axis=-1)
        a2 = jnp.concatenate([-pw_im[L, d], pw_im[L, d]], axis=-1)
        return [a1, a2]

    ab = jnp.stack(step_coeffs(0) + step_coeffs(1), axis=1)
    ab = jnp.broadcast_to(ab[:, :, None, :], ab.shape[:2] + (nb, 128))
    return wcat.astype(BF16), tfb.astype(BF16), cp.astype(BF16), ab


def _rwkv_kernel(*refs, nc):
    pre_f32, pre_bf16, pre_w = refs[-3:]
    s_scr = refs[-4]
    i = pl.program_id(0)

    @pl.when(i == 0)
    def _():
        pre_f32[1] = jnp.zeros(pre_f32.shape[1:], F32)
        pre_bf16[1] = jnp.zeros(pre_bf16.shape[1:], BF16)
        pre_w[1] = jnp.zeros(pre_w.shape[1:], F32)
        s_scr[...] = jnp.zeros(s_scr.shape, F32)

    for parity in range(2):
        @pl.when(i % 2 == parity)
        def _():
            _rwkv_step(*refs, nc=nc, slot_new=parity)


def _rwkv_step(rf_ref, vf_ref, kkf_ref, rb_ref, vb_ref, kkb_ref, kf_ref, kb_ref, lwf_ref, lwb_ref,
               icf_ref, icb_ref, s0_ref, yf_ref, yb_ref, sT_ref, s_scr, pre_f32, pre_bf16, pre_w, *, nc, slot_new):
    slot_cur = 1 - slot_new
    c = jnp.maximum(pl.program_id(0) - 1, 0) % nc
    L, N = RWKV_CHUNK, RWKV_HEAD
    P = 2 * N
    n_pairs = RWKV_WIDTH // P

    row = lax.broadcasted_iota(jnp.int32, (L, P), 0)
    lane = lax.broadcasted_iota(jnp.int32, (L, P), 1)
    col = lane % N
    first = lane < N
    eye = (col == row).astype(F32)
    strict = (col < row, col > row)
    incl = (col <= row, col >= row)
    row1 = lax.broadcasted_iota(jnp.int32, (L, L), 0)
    col1 = lax.broadcasted_iota(jnp.int32, (L, L), 1)
    tri = (col1 <= row1, col1 >= row1)

    def bd(y):
        return jnp.concatenate([jnp.where(first, y, 0).astype(BF16), jnp.where(first, 0, y).astype(BF16)], axis=0)

    def pmm(x, y):
        return _dot(x.astype(BF16), bd(y))

    def pmm_tn(x, y):
        full = _dot(x.T.astype(BF16), y.astype(BF16))
        return jnp.where(first, full[:N], full[N:])

    bt = rf_ref.shape[0]
    per_dir = ((rf_ref, vf_ref, kkf_ref, kf_ref, lwf_ref, icf_ref), (rb_ref, vb_ref, kkb_ref, kb_ref, lwb_ref, icb_ref))
    for bl in range(bt):
        for d, (r_ref, v_ref, kk_ref, k_ref, lw_ref, ic_ref) in enumerate(per_dir):
            lw = lw_ref[bl]
            cum = _dot_left01(jnp.where(tri[d], 1.0, 0.0).astype(BF16), lw)
            tot = jnp.sum(lw, axis=0, keepdims=True)
            kk = kk_ref[bl]
            b = kk * ic_ref[bl]
            k = k_ref[bl]
            e_ninc = jnp.exp(-cum)
            e_rem = jnp.exp(tot - cum)
            pre_f32[slot_new, bl, d, 0] = -kk * jnp.exp(cum - lw)
            pre_f32[slot_new, bl, d, 1] = r_ref[bl] * jnp.exp(cum)
            pre_f32[slot_new, bl, d, 2] = v_ref[bl]
            pre_bf16[slot_new, bl, d, 0] = (b * e_ninc).astype(BF16)
            pre_bf16[slot_new, bl, d, 1] = (k * e_ninc).astype(BF16)
            pre_bf16[slot_new, bl, d, 2] = (b * e_rem).astype(BF16)
            pre_bf16[slot_new, bl, d, 3] = (k * e_rem).astype(BF16)
            pre_w[slot_new, bl, d] = jnp.exp(tot)

    chains = [(bl, d, p) for bl in range(bt) for d in range(2) for p in range(n_pairs)]

    def pair(scr, j):
        return [scr[slot_cur, bl, d, j, :, p * P:(p + 1) * P] for bl, d, p in chains]

    a_tp, r_tp, vp = (pair(pre_f32, j) for j in range(3))
    b_tp, k_tp, b_hp, k_hp = (pair(pre_bf16, j) for j in range(4))
    w_p = [pre_w[slot_cur, bl, d, :, p * P:(p + 1) * P] for bl, d, p in chains]
    def rows2(x, y):
        return jnp.concatenate([x, y], axis=0)

    def cols2(x, y):
        return jnp.concatenate([x, y], axis=1)

    ar = [rows2(a, r).astype(BF16) for a, r in zip(a_tp, r_tp)]
    gram = [_dot_nt(x, rows2(bd(y), bd(z))) for x, y, z in zip(ar, b_tp, k_tp)]
    n_ab = [jnp.where(strict[d], g[:L, :P], 0.0) for (_, d, _), g in zip(chains, gram)]
    a_ak = [jnp.where(strict[d], g[:L, P:], 0.0) for (_, d, _), g in zip(chains, gram)]
    a_rb = [jnp.where(incl[d], g[L:, :P], 0.0).astype(BF16) for (_, d, _), g in zip(chains, gram)]
    a_rk = [jnp.where(incl[d], g[L:, P:], 0.0) for (_, d, _), g in zip(chains, gram)]
    kv = [pmm(rows2(x, y), z) for x, y, z in zip(a_ak, a_rk, vp)]
    inv = [eye + x for x in n_ab]
    pw = [pmm(x, x) for x in n_ab]
    for _ in range(4):
        both = [pmm(rows2(x, t), x) for x, t in zip(pw, inv)]
        inv = [t + z[L:] for t, z in zip(inv, both)]
        pw = [z[:L] for z in both]
    inv = [t + pmm(t, x) for t, x in zip(inv, pw)]
    invb = [x.astype(BF16) for x in inv]
    apvp = [_dot(t, cols2(bd(a), bd(z[:L]))) for t, a, z in zip(invb, a_tp, kv)]
    a_p = [z[:, :P] for z in apvp]
    v_p = [z[:, P:] for z in apvp]
    rb = [_dot(x, cols2(bd(y), bd(z))) for x, y, z in zip(a_rb, a_p, v_p)]
    r_p = [r + z[:, :P] for r, z in zip(r_tp, rb)]
    y_loc = [z[:, P:] + w[L:] for z, w in zip(rb, kv)]
    tn = [_dot(z.T.astype(BF16), y) for z, y in zip(apvp, b_hp)]
    p_lr = [jnp.where(first, z[:N], z[N:P]) for z in tn]
    q = [jnp.where(first, z[P:P + N], z[P + N:]) + pmm_tn(x, y) for z, x, y in zip(tn, vp, k_hp)]

    s = [jnp.where(c == 0, s0_ref[d, bl, p], s_scr[bl, d, p]) for bl, d, p in chains]
    s_bd = [bd(x) for x in s]
    y = [_dot_nt(x.astype(BF16), z) + yl for x, z, yl in zip(r_p, s_bd, y_loc)]
    s_new = [x * w + pmm(x, pl_) + qq for x, w, pl_, qq in zip(s, w_p, p_lr, q)]
    for (bl, d, p), yy, ss in zip(chains, y, s_new):
        (yf_ref, yb_ref)[d][bl, :, p * P:(p + 1) * P] = yy
        s_scr[bl, d, p] = ss
        sT_ref[d, bl, p] = ss


def _rwkv_call(r, v, kk, k2, lw2, ic2, s0):
    nb, n, w = r.shape
    bt = RWKV_BATCH
    nc = n // RWKV_CHUNK
    items = nb // bt * nc

    def fwd_chunk(item):
        return item % nc

    def bwd_chunk(item):
        return nc - 1 - item % nc

    def item_in(i):
        return jnp.minimum(i, items - 1)

    def item_out(i):
        return jnp.maximum(i - 1, 0)

    def tok(chunk, item):
        return pl.BlockSpec((bt, RWKV_CHUNK, w), lambda i: (item(i) // nc, chunk(item(i)), 0))

    both = [pl.BlockSpec((None, bt, RWKV_CHUNK, w), lambda i: (0, item_in(i) // nc, fwd_chunk(item_in(i)), 0)),
            pl.BlockSpec((None, bt, RWKV_CHUNK, w), lambda i: (1, item_in(i) // nc, bwd_chunk(item_in(i)), 0))]
    st = pl.BlockSpec((2, bt) + RWKV_STATE, lambda i: (0, item_out(i) // nc, 0, 0, 0))
    fwd_in, bwd_in = tok(fwd_chunk, item_in), tok(bwd_chunk, item_in)
    out = jax.ShapeDtypeStruct((nb, n, w), F32)
    return pl.pallas_call(
        functools.partial(_rwkv_kernel, nc=nc),
        grid=(items + 1,),
        in_specs=[fwd_in, fwd_in, fwd_in, bwd_in, bwd_in, bwd_in, *both, *both, *both, st],
        out_specs=[tok(fwd_chunk, item_out), tok(bwd_chunk, item_out), st],
        out_shape=[out, out, jax.ShapeDtypeStruct((2, nb) + RWKV_STATE, F32)],
        scratch_shapes=[pltpu.VMEM((bt, 2) + RWKV_STATE, F32),
                        pltpu.VMEM((2, bt, 2, 3, RWKV_CHUNK, w), F32),
                        pltpu.VMEM((2, bt, 2, 4, RWKV_CHUNK, w), BF16),
                        pltpu.VMEM((2, bt, 2, 1, w), F32)],
        compiler_params=_cparams("arbitrary"),
        name="rwkv",
    )(r, v, kk, r, v, kk, k2, k2, lw2, lw2, ic2, ic2, s0)


def _out_kernel(x_ref, mod_ref, ya_ref, u_ref, ybf_ref, ybb_ref, bonus_ref, gg_ref, sga_ref, sgb_ref,
                s5d_ref, wglu_ref, wproj_ref, lng_ref, lnb_ref, seg_ref, wo_ref, wout_ref, o_ref, ya_scr):
    nb, tn, _ = x_ref.shape

    per_tile = LANE // S5_GROUP
    for cl in range(tn // S5_CHUNK):
        for q in range(S5_WIDTH // LANE):
            per_g = [ya_ref[q * per_tile + gq, cl * nb:(cl + 1) * nb, :] for gq in range(per_tile)]
            for t in range(S5_CHUNK):
                sl = slice(t * S5_GROUP, (t + 1) * S5_GROUP)
                ya_scr[q, pl.ds(cl * S5_CHUNK + t, nb, stride=tn), :] = jnp.concatenate(
                    [y[:, sl] for y in per_g], axis=1)
    ya_tok = jnp.concatenate([ya_scr[q] for q in range(S5_WIDTH // LANE)], axis=1)
    rows = nb * tn

    def get(ref):
        return ref[...].reshape(rows, ref.shape[-1]).astype(F32)

    ya = jax.nn.gelu(ya_tok + s5d_ref[...] * get(u_ref))
    ya = ya * jax.nn.sigmoid(_dot(ya.astype(BF16), wglu_ref[...]))
    pa = _dot(ya.astype(BF16), wproj_ref[...])

    seg = seg_ref[...]
    yb = get(ybf_ref) + get(ybb_ref)
    mu = _dot_right01(yb, seg) * (1.0 / RWKV_HEAD)
    cen = yb - mu
    var = _dot_right01(cen * cen, seg) * (1.0 / RWKV_HEAD)
    yb = cen * lax.rsqrt(var + LN_X_EPS) * lng_ref[...] + lnb_ref[...] + get(bonus_ref)
    pb = _dot((yb * get(gg_ref)).astype(BF16), wo_ref[...])

    merged = get(sga_ref) * pa + get(sgb_ref) * pb
    out = _dot(merged.astype(BF16), wout_ref[...]).reshape(nb, tn, D_MODEL)
    o_ref[...] = x_ref[...] + mod_ref[:, 5:6] * out


def _out_call(x, mods, ya, u, ybf, ybb, bonus, gg, sga, sgb, s5d, wglu, wproj, lng, lnb, seg, wo, wout):
    nb, n, _ = x.shape
    tn = BATCH_TILE
    w = RWKV_WIDTH

    def tok(width):
        return pl.BlockSpec((nb, tn, width), lambda i: (0, i, 0))

    s5_rows = tn // S5_CHUNK * nb
    return pl.pallas_call(
        _out_kernel,
        grid=(n // tn,),
        in_specs=[tok(D_MODEL), _resident((nb, N_MOD, D_MODEL)),
                  pl.BlockSpec((S5_GROUPS, s5_rows, S5_CHUNK * S5_GROUP), lambda i: (0, i, 0)), tok(w),
                  tok(w), tok(w), tok(w), tok(w), tok(D_MODEL), tok(D_MODEL),
                  _resident((1, w)), _resident((w, w)), _resident((w, D_MODEL)), _resident((1, w)),
                  _resident((1, w)), _resident((w, w)), _resident((w, D_MODEL)), _resident((D_MODEL, D_MODEL))],
        out_specs=tok(D_MODEL),
        out_shape=jax.ShapeDtypeStruct((nb, n, D_MODEL), F32),
        scratch_shapes=[pltpu.VMEM((S5_WIDTH // LANE, nb * tn, LANE), F32)],
        compiler_params=_cparams("parallel"),
        name="outk",
    )(x, mods, ya, u, ybf, ybb, bonus, gg, sga, sgb, s5d, wglu, wproj, lng, lnb, seg, wo, wout)


def _block_diag2(a):
    z = jnp.zeros_like(a[0])
    return jnp.concatenate([jnp.concatenate([a[0], z], axis=1), jnp.concatenate([z, a[1]], axis=1)], axis=0)


def kernel(x, c, ctx, c_ctx, w_mod, b_mod, norm_g, ffn_w_gate, ffn_w_up, ffn_w_down, w_in, s5_A_re, s5_A_im, s5_log_dt, s5_B_re, s5_B_im, s5_C_re, s5_C_im, s5_D, s5_w_glu, s5_w_proj, rwkv_conv, rwkv_w0, rwkv_w2, rwkv_a0, rwkv_a2, rwkv_g2, rwkv_k_k, rwkv_k_a, rwkv_r_k, rwkv_ln_g, rwkv_ln_b, rwkv_w_o, w_out, final_g):
    nb, seq, d = x.shape
    n_ctx = ctx.shape[1]
    l = 0
    w = RWKV_WIDTH

    mod_rows = 16
    cc = jnp.concatenate([c, c_ctx[None], jnp.zeros((mod_rows - nb - 1, d), F32)], axis=0)
    mods = _mod_call(cc, w_mod[l], b_mod[l][None]).reshape(mod_rows, N_MOD, d)

    xt = x.reshape(nb * seq, d)
    ct = ctx.reshape(nb * n_ctx, d)
    row = lambda a: a.reshape(1, -1)
    bf = lambda a: a.astype(BF16)

    f1 = (row(norm_g[l, 0]), bf(ffn_w_gate[l, 0]), bf(ffn_w_up[l, 0]), bf(ffn_w_down[l, 0]), row(final_g))
    x1 = _ffn_call(xt, mods, *f1, j=0, tokens_per_batch=seq, shared_row=None, final_norm=False)
    c1 = _ffn_call(ct, mods, *f1, j=0, tokens_per_batch=n_ctx, shared_row=nb, final_norm=False)

    ip = (row(norm_g[l, 1]), bf(w_in[l]), bf(_block_diag2(rwkv_w2[l])), row(rwkv_w0[l]),
          bf(_block_diag2(rwkv_a2[l])), row(rwkv_a0[l]), bf(rwkv_g2[l]))
    mods_x = mods[:nb]
    mods_c = jnp.broadcast_to(mods[nb:nb + 1], (nb, N_MOD, d))
    u_x, us5_x, rkv_x, lw_x, ic_x, gg_x, sga_x, sgb_x = _inproj_call(x1.reshape(nb, seq, d), mods_x, *ip)
    _, us5_c, rkv_c, lw_c, ic_c, _, _, _ = _inproj_call(c1.reshape(nb, n_ctx, d), mods_c, *ip)

    def flat(a):
        return a.reshape(a.shape[:-3] + (a.shape[-3] * a.shape[-2], a.shape[-1]))

    s5p = _s5_matrices(s5_A_re[l], s5_A_im[l], s5_log_dt[l], s5_B_re[l], s5_B_im[l], s5_C_re[l], s5_C_im[l], nb)
    ya_s5 = _s5_call(us5_c, us5_x, *s5p, nb=nb)

    seg = (jnp.arange(w)[:, None] // RWKV_HEAD == jnp.arange(w)[None, :] // RWKV_HEAD).astype(BF16)
    pp = (rwkv_conv[l].reshape(9, 3 * w), row(rwkv_k_k[l]), row(rwkv_k_a[l]), row(rwkv_r_k[l]), seg)
    r_c, v_c, kk_c, k2_c, _ = _prep_call(flat(rkv_c), flat(ic_c), *pp, tokens_per_batch=n_ctx, width=n_ctx,
                                         vertical=False)
    r_x, v_x, kk_x, k2_x, bonus = _prep_call(flat(rkv_x), flat(ic_x), *pp, tokens_per_batch=seq, width=GRID_W,
                                             vertical=True)

    def by_batch(a, n):
        return a.reshape(a.shape[:-2] + (nb, n, a.shape[-1]))

    s0 = jnp.zeros((2, nb) + RWKV_STATE, F32)
    _, _, s_ctx = _rwkv_call(*(by_batch(a, n_ctx) for a in (r_c, v_c, kk_c, k2_c)), lw_c, ic_c, s0)
    ybf, ybb, _ = _rwkv_call(*(by_batch(a, seq) for a in (r_x, v_x, kk_x, k2_x)), lw_x, ic_x, s_ctx)

    x2 = _out_call(by_batch(x1, seq), mods_x, ya_s5, u_x, ybf, ybb, by_batch(bonus, seq), gg_x, sga_x,
                   sgb_x, row(s5_D[l]), bf(s5_w_glu[l]), bf(s5_w_proj[l]), row(rwkv_ln_g[l]), row(rwkv_ln_b[l]),
                   seg, bf(rwkv_w_o[l]), bf(w_out[l]))

    f2 = (row(norm_g[l, 2]), bf(ffn_w_gate[l, 1]), bf(ffn_w_up[l, 1]), bf(ffn_w_down[l, 1]), row(final_g))
    out = _ffn_call(flat(x2), mods, *f2, j=2, tokens_per_batch=seq, shared_row=None, final_norm=True)
    return out.reshape(nb, seq, d)
```

```python
import functools
import math

import jax
import jax.numpy as jnp
from jax import lax
from jax.experimental import pallas as pl
from jax.experimental.pallas import tpu as pltpu

F32 = jnp.float32
BF16 = jnp.bfloat16

D_MODEL = 1024
N_MOD = 9
FFN_DIM = 2816
RMS_EPS = 1e-6
GRID_W = 64
S5_WIDTH = 512
S5_GROUP = 16
S5_GROUPS = 32
S5_STATE = 64
RWKV_WIDTH = 512
RWKV_HEAD = 64
RWKV_HEADS = 8
LN_X_EPS = 64e-5
LANE = 128

S5_CHUNK = 16
RWKV_CHUNK = 64
TOKEN_TILE = 512
BATCH_TILE = 64
S5_SCAN_UNROLL = 4
ROW_SPLIT = 2
RWKV_STATE = (RWKV_HEADS // 2, RWKV_HEAD, 2 * RWKV_HEAD)
RWKV_BATCH = 2
VMEM_LIMIT = 56 * 1024 * 1024


def _cparams(*sem):
    return pltpu.CompilerParams(dimension_semantics=sem, vmem_limit_bytes=VMEM_LIMIT)


def _resident(shape):
    nd = len(shape)
    return pl.BlockSpec(shape, lambda *_: (0,) * nd, pipeline_mode=pl.Buffered(1))


def _dot(a, b):
    return jnp.dot(a, b, preferred_element_type=F32)


def _dot_nt(a, b):
    return lax.dot_general(a, b, (((1,), (1,)), ((), ())), preferred_element_type=F32)


def _split2(x):
    hi = x.astype(BF16)
    return hi, (x - hi.astype(F32)).astype(BF16)


def _dot_left01(m01, x):
    hi, lo = _split2(x)
    return _dot(m01, hi) + _dot(m01, lo)


def _dot_right01(x, m01):
    hi, lo = _split2(x)
    return _dot(hi, m01) + _dot(lo, m01)


def _rms_mod(x, g, shift, scale):
    y = x * lax.rsqrt(jnp.mean(x * x, axis=-1, keepdims=True) + RMS_EPS)
    return (y * g) * (1.0 + scale) + shift


def _mod_kernel(c_ref, w_ref, b_ref, o_ref):
    c = c_ref[...]
    s = c * jax.nn.sigmoid(c)
    o_ref[...] = jnp.dot(s, w_ref[...], precision=lax.Precision.HIGHEST, preferred_element_type=F32) + b_ref[...]


def _mod_call(cc, w_mod, b_mod):
    rows = cc.shape[0]
    return pl.pallas_call(
        _mod_kernel,
        grid=(N_MOD,),
        in_specs=[pl.BlockSpec((rows, D_MODEL), lambda j: (0, 0)),
                  pl.BlockSpec((D_MODEL, D_MODEL), lambda j: (0, j)),
                  pl.BlockSpec((1, D_MODEL), lambda j: (0, j))],
        out_specs=pl.BlockSpec((rows, D_MODEL), lambda j: (0, j)),
        out_shape=jax.ShapeDtypeStruct((rows, N_MOD * D_MODEL), F32),
        compiler_params=_cparams("arbitrary"),
        name="mod",
    )(cc, w_mod, b_mod)


def _ffn_kernel(x_ref, mod_ref, g_ref, wg_ref, wu_ref, wd_ref, fg_ref, o_ref, *, j, final_norm):
    x = x_ref[...]
    m = mod_ref[0]
    h = _rms_mod(x, g_ref[...], m[3 * j:3 * j + 1], m[3 * j + 1:3 * j + 2]).astype(BF16)
    a = _dot(h, wg_ref[...])
    a = (a * jax.nn.sigmoid(a)) * _dot(h, wu_ref[...])
    y = _dot(a.astype(BF16), wd_ref[...])
    out = x + 0.5 * m[3 * j + 2:3 * j + 3] * y
    if final_norm:
        out = out * lax.rsqrt(jnp.mean(out * out, axis=-1, keepdims=True) + RMS_EPS) * fg_ref[...]
    o_ref[...] = out


def _mod_spec(n_tiles, tiles_per_batch, shared_row):
    if shared_row is not None:
        return pl.BlockSpec((1, N_MOD, D_MODEL), lambda i: (shared_row, 0, 0))
    assert n_tiles % tiles_per_batch == 0
    return pl.BlockSpec((1, N_MOD, D_MODEL), lambda i: (i // tiles_per_batch, 0, 0))


def _ffn_call(x, mods, g, wg, wu, wd, fg, *, j, tokens_per_batch, shared_row, final_norm):
    n = x.shape[0]
    tm = min(TOKEN_TILE, tokens_per_batch)
    tok = pl.BlockSpec((tm, D_MODEL), lambda i: (i, 0))
    return pl.pallas_call(
        functools.partial(_ffn_kernel, j=j, final_norm=final_norm),
        grid=(n // tm,),
        in_specs=[tok, _mod_spec(n // tm, tokens_per_batch // tm, shared_row), _resident((1, D_MODEL)),
                  _resident((D_MODEL, FFN_DIM)), _resident((D_MODEL, FFN_DIM)), _resident((FFN_DIM, D_MODEL)),
                  _resident((1, D_MODEL))],
        out_specs=tok,
        out_shape=jax.ShapeDtypeStruct((n, D_MODEL), F32),
        compiler_params=_cparams("parallel"),
        name="ffn",
    )(x, mods, g, wg, wu, wd, fg)


O_U = S5_WIDTH
O_RKV = O_U + 3 * RWKV_WIDTH
O_WD = O_RKV + 128
O_AD = O_WD + 128
O_GD = O_AD + 128
O_GA = O_GD + D_MODEL
IN_COLS = O_GA + D_MODEL
DECAY_SCALE = math.exp(-0.5)


def _inproj_kernel(x_ref, mod_ref, g_ref, w_ref, w2_ref, w0_ref, a2_ref, a0_ref, g2_ref,
                   u_ref, us5_ref, rkv_ref, lw_ref, ic_ref, ics_ref, gg_ref, sga_ref, sgb_ref, u_scr):
    nb, tn, _ = x_ref.shape
    hb = nb // ROW_SPLIT
    for part in range(ROW_SPLIT):
        bs = slice(part * hb, (part + 1) * hb)
        rows = hb * tn
        m = mod_ref[bs]
        h = _rms_mod(x_ref[bs], g_ref[...], m[:, 3:4], m[:, 4:5]).reshape(rows, D_MODEL).astype(BF16)
        proj = _dot(h, w_ref[...])

        def put(ref, val):
            ref[bs] = val.reshape(hb, tn, val.shape[-1]).astype(ref.dtype)

        u = proj[:, :O_U]
        put(u_ref, u)
        put(rkv_ref, proj[:, O_U:O_RKV])
        w_log = w0_ref[...] + _dot(jnp.tanh(proj[:, O_RKV:O_WD]).astype(BF16), w2_ref[...])
        lw = -DECAY_SCALE * jax.nn.sigmoid(w_log)
        ic = jax.nn.sigmoid(a0_ref[...] + _dot(proj[:, O_WD:O_AD].astype(BF16), a2_ref[...]))
        for d in range(2):
            lw_ref[d, bs] = lw[:, d * RWKV_WIDTH:(d + 1) * RWKV_WIDTH].reshape(hb, tn, RWKV_WIDTH)
            ic_ref[d, bs] = ic[:, d * RWKV_WIDTH:(d + 1) * RWKV_WIDTH].reshape(hb, tn, RWKV_WIDTH)
        put(ics_ref, ic[:, :RWKV_WIDTH] + ic[:, RWKV_WIDTH:])
        put(gg_ref, _dot(jax.nn.sigmoid(proj[:, O_AD:O_GD]).astype(BF16), g2_ref[...]))
        put(sga_ref, jax.nn.sigmoid(proj[:, O_GD:O_GA]))
        put(sgb_ref, jax.nn.sigmoid(proj[:, O_GA:]))
        for q in range(S5_WIDTH // LANE):
            u_scr[q, part * rows:(part + 1) * rows] = u[:, q * LANE:(q + 1) * LANE]

    per_tile = LANE // S5_GROUP
    for q in range(S5_WIDTH // LANE):
        taps = [[u_scr[q, pl.ds(cl * S5_CHUNK + t, nb, stride=tn), :] for t in range(S5_CHUNK)]
                for cl in range(tn // S5_CHUNK)]
        for gq in range(per_tile):
            sl = slice(gq * S5_GROUP, (gq + 1) * S5_GROUP)
            rows_g = [jnp.concatenate([a[:, sl] for a in chunk_taps], axis=1) for chunk_taps in taps]
            us5_ref[q * per_tile + gq] = jnp.concatenate(rows_g, axis=0).astype(BF16)


def _inproj_call(x, mods, g, w_in, w2c, w0c, a2c, a0c, g2):
    nb, n, _ = x.shape
    tn = BATCH_TILE
    w = RWKV_WIDTH
    lw5 = S5_CHUNK * S5_GROUP

    def tok(width):
        return pl.BlockSpec((nb, tn, width), lambda i: (0, i, 0))

    tok2 = pl.BlockSpec((2, nb, tn, w), lambda i: (0, 0, i, 0))

    def shp(width, dtype=F32):
        return jax.ShapeDtypeStruct((nb, n, width), dtype)

    shp2 = jax.ShapeDtypeStruct((2, nb, n, w), F32)
    s5_rows = tn // S5_CHUNK * nb
    return pl.pallas_call(
        _inproj_kernel,
        grid=(n // tn,),
        in_specs=[tok(D_MODEL), _resident((nb, N_MOD, D_MODEL)), _resident((1, D_MODEL)),
                  _resident((D_MODEL, IN_COLS)), _resident((128, 2 * RWKV_WIDTH)), _resident((1, 2 * RWKV_WIDTH)),
                  _resident((128, 2 * RWKV_WIDTH)), _resident((1, 2 * RWKV_WIDTH)), _resident((128, RWKV_WIDTH))],
        out_specs=[tok(S5_WIDTH), pl.BlockSpec((S5_GROUPS, s5_rows, lw5), lambda i: (0, i, 0)), tok(3 * w),
                   tok2, tok2, tok(w), tok(w), tok(D_MODEL), tok(D_MODEL)],
        out_shape=[shp(S5_WIDTH), jax.ShapeDtypeStruct((S5_GROUPS, n // S5_CHUNK * nb, lw5), BF16), shp(3 * w),
                   shp2, shp2, shp(w), shp(w), shp(D_MODEL, BF16), shp(D_MODEL, BF16)],
        scratch_shapes=[pltpu.VMEM((S5_WIDTH // LANE, nb * tn, LANE), F32)],
        compiler_params=_cparams("parallel"),
        name="inproj",
    )(x, mods, g, w_in, w2c, w0c, a2c, a0c, g2)


def _prep_kernel(up_ref, mid_ref, dn_ref, ics_ref, cw_ref, kk_w_ref, ka_ref, rk_ref, seg_ref,
                 r_ref, v_ref, kk_ref, k_ref, bonus_ref, *, width, vertical, tiles_per_image):
    tm = mid_ref.shape[0]
    cw = cw_ref[...]
    mid = mid_ref[...]
    xpos = lax.broadcasted_iota(jnp.int32, (tm, 1), 0) % width
    not_first = xpos != 0
    not_last = xpos != width - 1

    rows = [(mid, 1)]
    if vertical:
        t = pl.program_id(0) % tiles_per_image
        up = jnp.where(t == 0, 0.0, up_ref[...])
        dn = jnp.where(t == tiles_per_image - 1, 0.0, dn_ref[...])
        ext = jnp.concatenate([up, mid, dn], axis=0)
        rows += [(ext[0:tm], 0), (ext[2 * width:2 * width + tm], 2)]

    def column(dx):
        return sum(base * cw[3 * dy + dx:3 * dy + dx + 1] for base, dy in rows)

    acc = (column(1) + jnp.where(not_first, pltpu.roll(column(0), 1, 0), 0.0)
           + jnp.where(not_last, pltpu.roll(column(2), tm - 1, 0), 0.0))

    w = RWKV_WIDTH
    r, k, v = acc[:, :w], acc[:, w:2 * w], acc[:, 2 * w:]
    seg = seg_ref[...]
    kk = k * kk_w_ref[...]
    kk = kk * lax.rsqrt(_dot_right01(kk * kk, seg) + 1e-12)
    r_ref[...] = r
    v_ref[...] = v
    kk_ref[...] = kk
    k_ref[...] = k
    k_sum = k * (2.0 + (ics_ref[...] - 2.0) * ka_ref[...])
    bonus_ref[...] = _dot_right01(r * rk_ref[...] * k_sum, seg) * v


def _prep_call(rkv, ics, cw, kk_w, ka, rk, seg, *, tokens_per_batch, width, vertical):
    n = rkv.shape[0]
    tm = min(TOKEN_TILE, tokens_per_batch)
    rows_per_tile = tm // width
    n_rows = n // width
    w3 = 3 * RWKV_WIDTH
    w = RWKV_WIDTH
    tok = pl.BlockSpec((tm, w), lambda i: (i, 0))
    return pl.pallas_call(
        functools.partial(_prep_kernel, width=width, vertical=vertical, tiles_per_image=tokens_per_batch // tm),
        grid=(n // tm,),
        in_specs=[pl.BlockSpec((width, w3), lambda i: (jnp.maximum(i * rows_per_tile - 1, 0), 0)),
                  pl.BlockSpec((tm, w3), lambda i: (i, 0)),
                  pl.BlockSpec((width, w3), lambda i: (jnp.minimum((i + 1) * rows_per_tile, n_rows - 1), 0)),
                  tok,
                  _resident((9, w3)), _resident((1, w)), _resident((1, w)), _resident((1, w)), _resident((w, w))],
        out_specs=[tok] * 5,
        out_shape=[jax.ShapeDtypeStruct((n, w), F32)] * 5,
        compiler_params=_cparams("parallel"),
        name="prep",
    )(rkv, rkv, rkv, ics, cw, kk_w, ka, rk, seg)


def _s5_kernel(uc_ref, ux_ref, wcat_ref, tfb_ref, cp_ref, ab_ref, y_ref, x_scr, h_scr, *, ctx_chunks, x_chunks, nb):
    ux = ux_ref[0]
    r0 = ctx_chunks * nb
    x_scr[:r0] = _dot(uc_ref[0], wcat_ref[0])
    x_scr[r0:] = _dot(ux, wcat_ref[0])
    ab = ab_ref[0]
    total = ctx_chunks + x_chunks

    def body(k, carry):
        hf, hfs, hb, hbs = carry
        cb = jnp.where(k < ctx_chunks, ctx_chunks - 1 - k, total + ctx_chunks - 1 - k)
        off_f = pl.multiple_of(k * nb, nb)
        off_b = pl.multiple_of(cb * nb, nb)
        h_scr[pl.ds(off_f, nb), 0:LANE] = hf
        h_scr[pl.ds(off_b, nb), LANE:2 * LANE] = hb
        xf, xfs = x_scr[pl.ds(off_f, nb), 0:LANE], x_scr[pl.ds(off_f, nb), LANE:2 * LANE]
        xb, xbs = x_scr[pl.ds(off_b, nb), 2 * LANE:3 * LANE], x_scr[pl.ds(off_b, nb), 3 * LANE:4 * LANE]
        return (ab[0] * hf + ab[1] * hfs + xf, ab[0] * hfs - ab[1] * hf + xfs,
                ab[2] * hb + ab[3] * hbs + xb, ab[2] * hbs - ab[3] * hb + xbs)

    zero = jnp.zeros((nb, LANE), F32)
    lax.fori_loop(0, total, body, (zero,) * 4, unroll=S5_SCAN_UNROLL)

    y_ref[0] = _dot(ux, tfb_ref[0]) + _dot(h_scr[r0:, :].astype(BF16), cp_ref[0])


def _s5_call(u_c, u_x, wcat, tfb, cp, ab, *, nb):
    g = u_x.shape[0]
    ctx_chunks = u_c.shape[1] // nb
    x_chunks = u_x.shape[1] // nb
    rows = (ctx_chunks + x_chunks) * nb
    lw = S5_CHUNK * S5_GROUP
    return pl.pallas_call(
        functools.partial(_s5_kernel, ctx_chunks=ctx_chunks, x_chunks=x_chunks, nb=nb),
        grid=(g,),
        in_specs=[pl.BlockSpec((1, ctx_chunks * nb, lw), lambda i: (i, 0, 0)),
                  pl.BlockSpec((1, x_chunks * nb, lw), lambda i: (i, 0, 0)),
                  pl.BlockSpec((1, lw, 512), lambda i: (i, 0, 0)),
                  pl.BlockSpec((1, lw, lw), lambda i: (i, 0, 0)),
                  pl.BlockSpec((1, 256, lw), lambda i: (i, 0, 0)),
                  pl.BlockSpec((1, 4, nb, 128), lambda i: (i, 0, 0, 0))],
        out_specs=pl.BlockSpec((1, x_chunks * nb, lw), lambda i: (i, 0, 0)),
        out_shape=jax.ShapeDtypeStruct((g, x_chunks * nb, lw), F32),
        scratch_shapes=[pltpu.VMEM((rows, 512), F32), pltpu.VMEM((rows, 256), F32)],
        compiler_params=_cparams("parallel"),
        name="s5",
    )(u_c, u_x, wcat, tfb, cp, ab)


def _s5_matrices(lam_re, lam_im, log_dt, b_re, b_im, c_re, c_im, nb):
    L, P, GS = S5_CHUNK, S5_STATE, S5_GROUP
    hp = lax.Precision.HIGHEST
    dt = jnp.exp(log_dt)[..., None]
    ab_re = jnp.exp(dt * lam_re) * jnp.cos(dt * lam_im)
    ab_im = jnp.exp(dt * lam_re) * jnp.sin(dt * lam_im)
    den = lam_re * lam_re + lam_im * lam_im
    z_re = ((ab_re - 1.0) * lam_re + ab_im * lam_im) / den
    z_im = (ab_im * lam_re - (ab_re - 1.0) * lam_im) / den
    bb_re = z_re[..., None] * b_re - z_im[..., None] * b_im
    bb_im = z_re[..., None] * b_im + z_im[..., None] * b_re
    k = jnp.arange(L + 1, dtype=F32)[:, None, None, None]
    pw_re = jnp.exp(k * dt * lam_re) * jnp.cos(k * dt * lam_im)
    pw_im = jnp.exp(k * dt * lam_re) * jnp.sin(k * dt * lam_im)
    ca_re = c_re[None] * pw_re[:, :, :, None, :] - c_im[None] * pw_im[:, :, :, None, :]
    ca_im = c_re[None] * pw_im[:, :, :, None, :] + c_im[None] * pw_re[:, :, :, None, :]
    ab_re_k = pw_re[..., None] * bb_re[None] - pw_im[..., None] * bb_im[None]
    ab_im_k = pw_re[..., None] * bb_im[None] + pw_im[..., None] * bb_re[None]
    kern = (jnp.einsum('kdgip,dgpj->dgkij', ca_re[:L], bb_re, precision=hp)
            - jnp.einsum('kdgip,dgpj->dgkij', ca_im[:L], bb_im, precision=hp))

    s = jnp.arange(L)[:, None]
    t = jnp.arange(L)[None, :]

    def toeplitz(kd, lag, valid):
        pick = ((lag[:, :, None] == jnp.arange(L)) & valid[:, :, None]).astype(F32)
        return jnp.einsum('stk,gkij->gsjti', pick, kd, precision=hp).reshape(-1, L * GS, L * GS)

    tfb = toeplitz(kern[0], t - s, t >= s) + toeplitz(kern[1], s - t, s >= t)

    def state_in(d, powers):
        wr = ab_re_k[powers, d].transpose(1, 0, 3, 2).reshape(-1, L * GS, P)
        wi = ab_im_k[powers, d].transpose(1, 0, 3, 2).reshape(-1, L * GS, P)
        return jnp.concatenate([wr, wi, wi, wr], axis=-1)

    wcat = jnp.concatenate([state_in(0, L - 1 - jnp.arange(L)), state_in(1, jnp.arange(L))], axis=-1)

    def state_out(d, powers):
        cr = ca_re[powers, d].transpose(1, 3, 0, 2).reshape(-1, P, L * GS)
        ci = ca_im[powers, d].transpose(1, 3, 0, 2).reshape(-1, P, L * GS)
        return jnp.concatenate([cr, -ci], axis=1)

    cp = jnp.concatenate([state_out(0, 1 + jnp.arange(L)), state_out(1, L - jnp.arange(L))], axis=1)

    def step_coeffs(d):
        a1 = jnp.concatenate([pw_re[L, d], pw_re[L, d]], axis=-1)
        a2 = jnp.concatenate([-pw_im[L, d], pw_im[L, d]], axis=-1)
        return [a1, a2]

    ab = jnp.stack(step_coeffs(0) + step_coeffs(1), axis=1)
    ab = jnp.broadcast_to(ab[:, :, None, :], ab.shape[:2] + (nb, 128))
    return wcat.astype(BF16), tfb.astype(BF16), cp.astype(BF16), ab


def _rwkv_kernel(*refs, nc):
    pre_f32, pre_bf16, pre_w = refs[-3:]
    s_scr = refs[-4]
    i = pl.program_id(0)

    @pl.when(i == 0)
    def _():
        pre_f32[1] = jnp.zeros(pre_f32.shape[1:], F32)
        pre_bf16[1] = jnp.zeros(pre_bf16.shape[1:], BF16)
        pre_w[1] = jnp.zeros(pre_w.shape[1:], F32)
        s_scr[...] = jnp.zeros(s_scr.shape, F32)

    for parity in range(2):
        @pl.when(i % 2 == parity)
        def _():
            _rwkv_step(*refs, nc=nc, slot_new=parity)


def _rwkv_step(rf_ref, vf_ref, kkf_ref, rb_ref, vb_ref, kkb_ref, kf_ref, kb_ref, lwf_ref, lwb_ref,
               icf_ref, icb_ref, s0_ref, ka_ref, yf_ref, yb_ref, sT_ref, s_scr, pre_f32, pre_bf16, pre_w,
               *, nc, slot_new):
    slot_cur = 1 - slot_new
    c = jnp.maximum(pl.program_id(0) - 1, 0) % nc
    L, N = RWKV_CHUNK, RWKV_HEAD
    P = 2 * N
    n_pairs = RWKV_WIDTH // P

    row = lax.broadcasted_iota(jnp.int32, (L, P), 0)
    lane = lax.broadcasted_iota(jnp.int32, (L, P), 1)
    col = lane % N
    first = lane < N
    eye = (col == row).astype(F32)
    strict = (col < row, col > row)
    incl = (col <= row, col >= row)
    row1 = lax.broadcasted_iota(jnp.int32, (L, L), 0)
    col1 = lax.broadcasted_iota(jnp.int32, (L, L), 1)
    tri = (col1 <= row1, col1 >= row1)

    def bd(y):
        return jnp.concatenate([jnp.where(first, y, 0).astype(BF16), jnp.where(first, 0, y).astype(BF16)], axis=0)

    def pmm(x, y):
        return _dot(x.astype(BF16), bd(y))

    def pmm_tn(x, y):
        full = _dot(x.T.astype(BF16), y.astype(BF16))
        return jnp.where(first, full[:N], full[N:])

    bt = rf_ref.shape[0]
    per_dir = ((rf_ref, vf_ref, kkf_ref, kf_ref, lwf_ref, icf_ref), (rb_ref, vb_ref, kkb_ref, kb_ref, lwb_ref, icb_ref))
    for bl in range(bt):
        for d, (r_ref, v_ref, kk_ref, k_ref, lw_ref, ic_ref) in enumerate(per_dir):
            lw = lw_ref[bl]
            cum = _dot_left01(jnp.where(tri[d], 1.0, 0.0).astype(BF16), lw)
            tot = jnp.sum(lw, axis=0, keepdims=True)
            kk = kk_ref[bl]
            ic = ic_ref[bl]
            b = kk * ic
            k = k_ref[bl] * (1.0 + (ic - 1.0) * ka_ref[...])
            e_ninc = jnp.exp(-cum)
            e_rem = jnp.exp(tot - cum)
            pre_f32[slot_new, bl, d, 0] = -kk * jnp.exp(cum - lw)
            pre_f32[slot_new, bl, d, 1] = r_ref[bl] * jnp.exp(cum)
            pre_f32[slot_new, bl, d, 2] = v_ref[bl]
            pre_bf16[slot_new, bl, d, 0] = (b * e_ninc).astype(BF16)
            pre_bf16[slot_new, bl, d, 1] = (k * e_ninc).astype(BF16)
            pre_bf16[slot_new, bl, d, 2] = (b * e_rem).astype(BF16)
            pre_bf16[slot_new, bl, d, 3] = (k * e_rem).astype(BF16)
            pre_w[slot_new, bl, d] = jnp.exp(tot)

    chains = [(bl, d, p) for bl in range(bt) for d in range(2) for p in range(n_pairs)]

    def pair(scr, j):
        return [scr[slot_cur, bl, d, j, :, p * P:(p + 1) * P] for bl, d, p in chains]

    a_tp, r_tp, vp = (pair(pre_f32, j) for j in range(3))
    b_tp, k_tp, b_hp, k_hp = (pair(pre_bf16, j) for j in range(4))
    w_p = [pre_w[slot_cur, bl, d, :, p * P:(p + 1) * P] for bl, d, p in chains]
    def rows2(x, y):
        return jnp.concatenate([x, y], axis=0)

    def cols2(x, y):
        return jnp.concatenate([x, y], axis=1)

    ar = [rows2(a, r).astype(BF16) for a, r in zip(a_tp, r_tp)]
    gram = [_dot_nt(x, rows2(bd(y), bd(z))) for x, y, z in zip(ar, b_tp, k_tp)]
    n_ab = [jnp.where(strict[d], g[:L, :P], 0.0) for (_, d, _), g in zip(chains, gram)]
    a_ak = [jnp.where(strict[d], g[:L, P:], 0.0) for (_, d, _), g in zip(chains, gram)]
    a_rb = [jnp.where(incl[d], g[L:, :P], 0.0).astype(BF16) for (_, d, _), g in zip(chains, gram)]
    a_rk = [jnp.where(incl[d], g[L:, P:], 0.0) for (_, d, _), g in zip(chains, gram)]
    kv = [pmm(rows2(x, y), z) for x, y, z in zip(a_ak, a_rk, vp)]
    inv = [eye + x for x in n_ab]
    pw = [pmm(x, x) for x in n_ab]
    for _ in range(4):
        both = [pmm(rows2(x, t), x) for x, t in zip(pw, inv)]
        inv = [t + z[L:] for t, z in zip(inv, both)]
        pw = [z[:L] for z in both]
    inv = [t + pmm(t, x) for t, x in zip(inv, pw)]
    invb = [x.astype(BF16) for x in inv]
    apvp = [_dot(t, cols2(bd(a), bd(z[:L]))) for t, a, z in zip(invb, a_tp, kv)]
    a_p = [z[:, :P] for z in apvp]
    v_p = [z[:, P:] for z in apvp]
    rb = [_dot(x, cols2(bd(y), bd(z))) for x, y, z in zip(a_rb, a_p, v_p)]
    r_p = [r + z[:, :P] for r, z in zip(r_tp, rb)]
    y_loc = [z[:, P:] + w[L:] for z, w in zip(rb, kv)]
    tn = [_dot(z.T.astype(BF16), y) for z, y in zip(apvp, b_hp)]
    p_lr = [jnp.where(first, z[:N], z[N:P]) for z in tn]
    q = [jnp.where(first, z[P:P + N], z[P + N:]) + pmm_tn(x, y) for z, x, y in zip(tn, vp, k_hp)]

    s = [jnp.where(c == 0, s0_ref[d, bl, p], s_scr[bl, d, p]) for bl, d, p in chains]
    s_bd = [bd(x) for x in s]
    y = [_dot_nt(x.astype(BF16), z) + yl for x, z, yl in zip(r_p, s_bd, y_loc)]
    s_new = [x * w + pmm(x, pl_) + qq for x, w, pl_, qq in zip(s, w_p, p_lr, q)]
    for (bl, d, p), yy, ss in zip(chains, y, s_new):
        (yf_ref, yb_ref)[d][bl, :, p * P:(p + 1) * P] = yy
        s_scr[bl, d, p] = ss
        sT_ref[d, bl, p] = ss


def _rwkv_call(r, v, kk, k, lw2, ic2, s0, ka):
    nb, n, w = r.shape
    bt = RWKV_BATCH
    nc = n // RWKV_CHUNK
    items = nb // bt * nc

    def fwd_chunk(item):
        return item % nc

    def bwd_chunk(item):
        return nc - 1 - item % nc

    def item_in(i):
        return jnp.minimum(i, items - 1)

    def item_out(i):
        return jnp.maximum(i - 1, 0)

    def tok(chunk, item):
        return pl.BlockSpec((bt, RWKV_CHUNK, w), lambda i: (item(i) // nc, chunk(item(i)), 0))

    both = [pl.BlockSpec((None, bt, RWKV_CHUNK, w), lambda i: (0, item_in(i) // nc, fwd_chunk(item_in(i)), 0)),
            pl.BlockSpec((None, bt, RWKV_CHUNK, w), lambda i: (1, item_in(i) // nc, bwd_chunk(item_in(i)), 0))]
    st = pl.BlockSpec((2, bt) + RWKV_STATE, lambda i: (0, item_out(i) // nc, 0, 0, 0))
    fwd_in, bwd_in = tok(fwd_chunk, item_in), tok(bwd_chunk, item_in)
    out = jax.ShapeDtypeStruct((nb, n, w), F32)
    return pl.pallas_call(
        functools.partial(_rwkv_kernel, nc=nc),
        grid=(items + 1,),
        in_specs=[fwd_in, fwd_in, fwd_in, bwd_in, bwd_in, bwd_in, fwd_in, bwd_in, *both, *both, st,
                  _resident((1, w))],
        out_specs=[tok(fwd_chunk, item_out), tok(bwd_chunk, item_out), st],
        out_shape=[out, out, jax.ShapeDtypeStruct((2, nb) + RWKV_STATE, F32)],
        scratch_shapes=[pltpu.VMEM((bt, 2) + RWKV_STATE, F32),
                        pltpu.VMEM((2, bt, 2, 3, RWKV_CHUNK, w), F32),
                        pltpu.VMEM((2, bt, 2, 4, RWKV_CHUNK, w), BF16),
                        pltpu.VMEM((2, bt, 2, 1, w), F32)],
        compiler_params=_cparams("arbitrary"),
        name="rwkv",
    )(r, v, kk, r, v, kk, k, k, lw2, lw2, ic2, ic2, s0, ka)


def _out_kernel(x_ref, mod_ref, ya_ref, u_ref, ybf_ref, ybb_ref, bonus_ref, gg_ref, sga_ref, sgb_ref,
                s5d_ref, wglu_ref, wproj_ref, lng_ref, lnb_ref, seg_ref, wo_ref, wout_ref, o_ref, ya_scr):
    nb, tn, _ = x_ref.shape

    per_tile = LANE // S5_GROUP
    for cl in range(tn // S5_CHUNK):
        for q in range(S5_WIDTH // LANE):
            per_g = [ya_ref[q * per_tile + gq, cl * nb:(cl + 1) * nb, :] for gq in range(per_tile)]
            for t in range(S5_CHUNK):
                sl = slice(t * S5_GROUP, (t + 1) * S5_GROUP)
                ya_scr[q, pl.ds(cl * S5_CHUNK + t, nb, stride=tn), :] = jnp.concatenate(
                    [y[:, sl] for y in per_g], axis=1)
    ya_tok = jnp.concatenate([ya_scr[q] for q in range(S5_WIDTH // LANE)], axis=1)
    rows = nb * tn

    def get(ref):
        return ref[...].reshape(rows, ref.shape[-1]).astype(F32)

    ya = jax.nn.gelu(ya_tok + s5d_ref[...] * get(u_ref))
    ya = ya * jax.nn.sigmoid(_dot(ya.astype(BF16), wglu_ref[...]))
    pa = _dot(ya.astype(BF16), wproj_ref[...])

    seg = seg_ref[...]
    yb = get(ybf_ref) + get(ybb_ref)
    mu = _dot_right01(yb, seg) * (1.0 / RWKV_HEAD)
    cen = yb - mu
    var = _dot_right01(cen * cen, seg) * (1.0 / RWKV_HEAD)
    yb = cen * lax.rsqrt(var + LN_X_EPS) * lng_ref[...] + lnb_ref[...] + get(bonus_ref)
    pb = _dot((yb * get(gg_ref)).astype(BF16), wo_ref[...])

    merged = get(sga_ref) * pa + get(sgb_ref) * pb
    out = _dot(merged.astype(BF16), wout_ref[...]).reshape(nb, tn, D_MODEL)
    o_ref[...] = x_ref[...] + mod_ref[:, 5:6] * out


def _out_call(x, mods, ya, u, ybf, ybb, bonus, gg, sga, sgb, s5d, wglu, wproj, lng, lnb, seg, wo, wout):
    nb, n, _ = x.shape
    tn = BATCH_TILE
    w = RWKV_WIDTH

    def tok(width):
        return pl.BlockSpec((nb, tn, width), lambda i: (0, i, 0))

    s5_rows = tn // S5_CHUNK * nb
    return pl.pallas_call(
        _out_kernel,
        grid=(n // tn,),
        in_specs=[tok(D_MODEL), _resident((nb, N_MOD, D_MODEL)),
                  pl.BlockSpec((S5_GROUPS, s5_rows, S5_CHUNK * S5_GROUP), lambda i: (0, i, 0)), tok(w),
                  tok(w), tok(w), tok(w), tok(w), tok(D_MODEL), tok(D_MODEL),
                  _resident((1, w)), _resident((w, w)), _resident((w, D_MODEL)), _resident((1, w)),
                  _resident((1, w)), _resident((w, w)), _resident((w, D_MODEL)), _resident((D_MODEL, D_MODEL))],
        out_specs=tok(D_MODEL),
        out_shape=jax.ShapeDtypeStruct((nb, n, D_MODEL), F32),
        scratch_shapes=[pltpu.VMEM((S5_WIDTH // LANE, nb * tn, LANE), F32)],
        compiler_params=_cparams("parallel"),
        name="outk",
    )(x, mods, ya, u, ybf, ybb, bonus, gg, sga, sgb, s5d, wglu, wproj, lng, lnb, seg, wo, wout)


def _block_diag2(a):
    z = jnp.zeros_like(a[0])
    return jnp.concatenate([jnp.concatenate([a[0], z], axis=1), jnp.concatenate([z, a[1]], axis=1)], axis=0)


def kernel(x, c, ctx, c_ctx, w_mod, b_mod, norm_g, ffn_w_gate, ffn_w_up, ffn_w_down, w_in, s5_A_re, s5_A_im, s5_log_dt, s5_B_re, s5_B_im, s5_C_re, s5_C_im, s5_D, s5_w_glu, s5_w_proj, rwkv_conv, rwkv_w0, rwkv_w2, rwkv_a0, rwkv_a2, rwkv_g2, rwkv_k_k, rwkv_k_a, rwkv_r_k, rwkv_ln_g, rwkv_ln_b, rwkv_w_o, w_out, final_g):
    nb, seq, d = x.shape
    n_ctx = ctx.shape[1]
    l = 0
    w = RWKV_WIDTH

    mod_rows = 16
    cc = jnp.concatenate([c, c_ctx[None], jnp.zeros((mod_rows - nb - 1, d), F32)], axis=0)
    mods = _mod_call(cc, w_mod[l], b_mod[l][None]).reshape(mod_rows, N_MOD, d)

    xt = x.reshape(nb * seq, d)
    ct = ctx.reshape(nb * n_ctx, d)
    row = lambda a: a.reshape(1, -1)
    bf = lambda a: a.astype(BF16)

    f1 = (row(norm_g[l, 0]), bf(ffn_w_gate[l, 0]), bf(ffn_w_up[l, 0]), bf(ffn_w_down[l, 0]), row(final_g))
    x1 = _ffn_call(xt, mods, *f1, j=0, tokens_per_batch=seq, shared_row=None, final_norm=False)
    c1 = _ffn_call(ct, mods, *f1, j=0, tokens_per_batch=n_ctx, shared_row=nb, final_norm=False)

    ip = (row(norm_g[l, 1]), bf(w_in[l]), bf(_block_diag2(rwkv_w2[l])), row(rwkv_w0[l]),
          bf(_block_diag2(rwkv_a2[l])), row(rwkv_a0[l]), bf(rwkv_g2[l]))
    mods_x = mods[:nb]
    mods_c = jnp.broadcast_to(mods[nb:nb + 1], (nb, N_MOD, d))
    u_x, us5_x, rkv_x, lw_x, ic_x, ics_x, gg_x, sga_x, sgb_x = _inproj_call(x1.reshape(nb, seq, d), mods_x, *ip)
    _, us5_c, rkv_c, lw_c, ic_c, ics_c, _, _, _ = _inproj_call(c1.reshape(nb, n_ctx, d), mods_c, *ip)

    def flat(a):
        return a.reshape(a.shape[:-3] + (a.shape[-3] * a.shape[-2], a.shape[-1]))

    s5p = _s5_matrices(s5_A_re[l], s5_A_im[l], s5_log_dt[l], s5_B_re[l], s5_B_im[l], s5_C_re[l], s5_C_im[l], nb)
    ya_s5 = _s5_call(us5_c, us5_x, *s5p, nb=nb)

    seg = (jnp.arange(w)[:, None] // RWKV_HEAD == jnp.arange(w)[None, :] // RWKV_HEAD).astype(BF16)
    pp = (rwkv_conv[l].reshape(9, 3 * w), row(rwkv_k_k[l]), row(rwkv_k_a[l]), row(rwkv_r_k[l]), seg)
    r_c, v_c, kk_c, k_c, _ = _prep_call(flat(rkv_c), flat(ics_c), *pp, tokens_per_batch=n_ctx, width=n_ctx,
                                        vertical=False)
    r_x, v_x, kk_x, k_x, bonus = _prep_call(flat(rkv_x), flat(ics_x), *pp, tokens_per_batch=seq, width=GRID_W,
                                            vertical=True)

    def by_batch(a, n):
        return a.reshape(nb, n, a.shape[-1])

    s0 = jnp.zeros((2, nb) + RWKV_STATE, F32)
    ka = row(rwkv_k_a[l])
    _, _, s_ctx = _rwkv_call(*(by_batch(a, n_ctx) for a in (r_c, v_c, kk_c, k_c)), lw_c, ic_c, s0, ka)
    ybf, ybb, _ = _rwkv_call(*(by_batch(a, seq) for a in (r_x, v_x, kk_x, k_x)), lw_x, ic_x, s_ctx, ka)

    x2 = _out_call(by_batch(x1, seq), mods_x, ya_s5, u_x, ybf, ybb, by_batch(bonus, seq), gg_x, sga_x,
                   sgb_x, row(s5_D[l]), bf(s5_w_glu[l]), bf(s5_w_proj[l]), row(rwkv_ln_g[l]), row(rwkv_ln_b[l]),
                   seg, bf(rwkv_w_o[l]), bf(w_out[l]))

    f2 = (row(norm_g[l, 2]), bf(ffn_w_gate[l, 1]), bf(ffn_w_up[l, 1]), bf(ffn_w_down[l, 1]), row(final_g))
    out = _ffn_call(flat(x2), mods, *f2, j=2, tokens_per_batch=seq, shared_row=None, final_norm=True)
    return out.reshape(nb, seq, d)
```

```python
import functools
import math

import jax
import jax.numpy as jnp
from jax import lax
from jax.experimental import pallas as pl
from jax.experimental.pallas import tpu as pltpu

F32 = jnp.float32
BF16 = jnp.bfloat16

D_MODEL = 1024
N_MOD = 9
FFN_DIM = 2816
RMS_EPS = 1e-6
GRID_W = 64
S5_WIDTH = 512
S5_GROUP = 16
S5_GROUPS = 32
S5_STATE = 64
RWKV_WIDTH = 512
RWKV_HEAD = 64
RWKV_HEADS = 8
LN_X_EPS = 64e-5
LANE = 128

S5_CHUNK = 16
RWKV_CHUNK = 64
TOKEN_TILE = 512
BATCH_TILE = 64
S5_SCAN_UNROLL = 4
ROW_SPLIT = 2
RWKV_STATE = (RWKV_HEADS // 2, RWKV_HEAD, 2 * RWKV_HEAD)
RWKV_BATCH = 4
VMEM_LIMIT = 56 * 1024 * 1024


def _cparams(*sem):
    return pltpu.CompilerParams(dimension_semantics=sem, vmem_limit_bytes=VMEM_LIMIT)


def _resident(shape):
    nd = len(shape)
    return pl.BlockSpec(shape, lambda *_: (0,) * nd, pipeline_mode=pl.Buffered(1))


def _dot(a, b):
    return jnp.dot(a, b, preferred_element_type=F32)


def _dot_nt(a, b):
    return lax.dot_general(a, b, (((1,), (1,)), ((), ())), preferred_element_type=F32)


def _split2(x):
    hi = x.astype(BF16)
    return hi, (x - hi.astype(F32)).astype(BF16)


def _dot_left01(m01, x):
    hi, lo = _split2(x)
    return _dot(m01, hi) + _dot(m01, lo)


def _dot_right01(x, m01):
    hi, lo = _split2(x)
    return _dot(hi, m01) + _dot(lo, m01)


def _rms_mod(x, g, shift, scale):
    y = x * lax.rsqrt(jnp.mean(x * x, axis=-1, keepdims=True) + RMS_EPS)
    return (y * g) * (1.0 + scale) + shift


def _mod_kernel(c_ref, w_ref, b_ref, o_ref):
    c = c_ref[...]
    s = c * jax.nn.sigmoid(c)
    o_ref[...] = jnp.dot(s, w_ref[...], precision=lax.Precision.HIGHEST, preferred_element_type=F32) + b_ref[...]


def _mod_call(cc, w_mod, b_mod):
    rows = cc.shape[0]
    return pl.pallas_call(
        _mod_kernel,
        grid=(N_MOD,),
        in_specs=[pl.BlockSpec((rows, D_MODEL), lambda j: (0, 0)),
                  pl.BlockSpec((D_MODEL, D_MODEL), lambda j: (0, j)),
                  pl.BlockSpec((1, D_MODEL), lambda j: (0, j))],
        out_specs=pl.BlockSpec((rows, D_MODEL), lambda j: (0, j)),
        out_shape=jax.ShapeDtypeStruct((rows, N_MOD * D_MODEL), F32),
        compiler_params=_cparams("arbitrary"),
        name="mod",
    )(cc, w_mod, b_mod)


def _ffn_kernel(x_ref, mod_ref, g_ref, wg_ref, wu_ref, wd_ref, fg_ref, o_ref, *, j, final_norm):
    x = x_ref[...]
    m = mod_ref[0]
    h = _rms_mod(x, g_ref[...], m[3 * j:3 * j + 1], m[3 * j + 1:3 * j + 2]).astype(BF16)
    a = _dot(h, wg_ref[...])
    a = (a * jax.nn.sigmoid(a)) * _dot(h, wu_ref[...])
    y = _dot(a.astype(BF16), wd_ref[...])
    out = x + 0.5 * m[3 * j + 2:3 * j + 3] * y
    if final_norm:
        out = out * lax.rsqrt(jnp.mean(out * out, axis=-1, keepdims=True) + RMS_EPS) * fg_ref[...]
    o_ref[...] = out


def _mod_spec(n_tiles, tiles_per_batch, shared_row):
    if shared_row is not None:
        return pl.BlockSpec((1, N_MOD, D_MODEL), lambda i: (shared_row, 0, 0))
    assert n_tiles % tiles_per_batch == 0
    return pl.BlockSpec((1, N_MOD, D_MODEL), lambda i: (i // tiles_per_batch, 0, 0))


def _ffn_call(x, mods, g, wg, wu, wd, fg, *, j, tokens_per_batch, shared_row, final_norm):
    n = x.shape[0]
    tm = TOKEN_TILE if shared_row is not None else min(TOKEN_TILE, tokens_per_batch)
    tok = pl.BlockSpec((tm, D_MODEL), lambda i: (i, 0))
    return pl.pallas_call(
        functools.partial(_ffn_kernel, j=j, final_norm=final_norm),
        grid=(n // tm,),
        in_specs=[tok, _mod_spec(n // tm, tokens_per_batch // tm, shared_row), _resident((1, D_MODEL)),
                  _resident((D_MODEL, FFN_DIM)), _resident((D_MODEL, FFN_DIM)), _resident((FFN_DIM, D_MODEL)),
                  _resident((1, D_MODEL))],
        out_specs=tok,
        out_shape=jax.ShapeDtypeStruct((n, D_MODEL), F32),
        compiler_params=_cparams("parallel"),
        name="ffn",
    )(x, mods, g, wg, wu, wd, fg)


O_U = S5_WIDTH
O_RKV = O_U + 3 * RWKV_WIDTH
O_WD = O_RKV + 128
O_AD = O_WD + 128
O_GD = O_AD + 128
O_GA = O_GD + D_MODEL
IN_COLS = O_GA + D_MODEL
DECAY_SCALE = math.exp(-0.5)


def _inproj_kernel(x_ref, mod_ref, g_ref, w_ref, w2_ref, w0_ref, a2_ref, a0_ref, g2_ref,
                   u_ref, us5_ref, rkv_ref, lw_ref, ic_ref, ics_ref, gg_ref, sga_ref, sgb_ref, u_scr):
    nb, tn, _ = x_ref.shape
    hb = nb // ROW_SPLIT
    for part in range(ROW_SPLIT):
        bs = slice(part * hb, (part + 1) * hb)
        rows = hb * tn
        m = mod_ref[bs]
        h = _rms_mod(x_ref[bs], g_ref[...], m[:, 3:4], m[:, 4:5]).reshape(rows, D_MODEL).astype(BF16)
        proj = _dot(h, w_ref[...])

        def put(ref, val):
            ref[bs] = val.reshape(hb, tn, val.shape[-1]).astype(ref.dtype)

        u = proj[:, :O_U]
        put(u_ref, u)
        put(rkv_ref, proj[:, O_U:O_RKV])
        w_log = w0_ref[...] + _dot(jnp.tanh(proj[:, O_RKV:O_WD]).astype(BF16), w2_ref[...])
        lw = -DECAY_SCALE * jax.nn.sigmoid(w_log)
        ic = jax.nn.sigmoid(a0_ref[...] + _dot(proj[:, O_WD:O_AD].astype(BF16), a2_ref[...]))
        for d in range(2):
            lw_ref[d, bs] = lw[:, d * RWKV_WIDTH:(d + 1) * RWKV_WIDTH].reshape(hb, tn, RWKV_WIDTH)
            ic_ref[d, bs] = ic[:, d * RWKV_WIDTH:(d + 1) * RWKV_WIDTH].reshape(hb, tn, RWKV_WIDTH)
        put(ics_ref, ic[:, :RWKV_WIDTH] + ic[:, RWKV_WIDTH:])
        put(gg_ref, _dot(jax.nn.sigmoid(proj[:, O_AD:O_GD]).astype(BF16), g2_ref[...]))
        put(sga_ref, jax.nn.sigmoid(proj[:, O_GD:O_GA]))
        put(sgb_ref, jax.nn.sigmoid(proj[:, O_GA:]))
        for q in range(S5_WIDTH // LANE):
            u_scr[q, part * rows:(part + 1) * rows] = u[:, q * LANE:(q + 1) * LANE]

    per_tile = LANE // S5_GROUP
    for q in range(S5_WIDTH // LANE):
        taps = [[u_scr[q, pl.ds(cl * S5_CHUNK + t, nb, stride=tn), :] for t in range(S5_CHUNK)]
                for cl in range(tn // S5_CHUNK)]
        for gq in range(per_tile):
            sl = slice(gq * S5_GROUP, (gq + 1) * S5_GROUP)
            rows_g = [jnp.concatenate([a[:, sl] for a in chunk_taps], axis=1) for chunk_taps in taps]
            us5_ref[q * per_tile + gq] = jnp.concatenate(rows_g, axis=0).astype(BF16)


def _inproj_call(x, mods, g, w_in, w2c, w0c, a2c, a0c, g2):
    nb, n, _ = x.shape
    tn = BATCH_TILE
    w = RWKV_WIDTH
    lw5 = S5_CHUNK * S5_GROUP

    def tok(width):
        return pl.BlockSpec((nb, tn, width), lambda i: (0, i, 0))

    tok2 = pl.BlockSpec((2, nb, tn, w), lambda i: (0, 0, i, 0))

    def shp(width, dtype=F32):
        return jax.ShapeDtypeStruct((nb, n, width), dtype)

    shp2 = jax.ShapeDtypeStruct((2, nb, n, w), F32)
    s5_rows = tn // S5_CHUNK * nb
    return pl.pallas_call(
        _inproj_kernel,
        grid=(n // tn,),
        in_specs=[tok(D_MODEL), _resident((nb, N_MOD, D_MODEL)), _resident((1, D_MODEL)),
                  _resident((D_MODEL, IN_COLS)), _resident((128, 2 * RWKV_WIDTH)), _resident((1, 2 * RWKV_WIDTH)),
                  _resident((128, 2 * RWKV_WIDTH)), _resident((1, 2 * RWKV_WIDTH)), _resident((128, RWKV_WIDTH))],
        out_specs=[tok(S5_WIDTH), pl.BlockSpec((S5_GROUPS, s5_rows, lw5), lambda i: (0, i, 0)), tok(3 * w),
                   tok2, tok2, tok(w), tok(w), tok(D_MODEL), tok(D_MODEL)],
        out_shape=[shp(S5_WIDTH), jax.ShapeDtypeStruct((S5_GROUPS, n // S5_CHUNK * nb, lw5), BF16), shp(3 * w),
                   shp2, shp2, shp(w), shp(w), shp(D_MODEL, BF16), shp(D_MODEL, BF16)],
        scratch_shapes=[pltpu.VMEM((S5_WIDTH // LANE, nb * tn, LANE), F32)],
        compiler_params=_cparams("parallel"),
        name="inproj",
    )(x, mods, g, w_in, w2c, w0c, a2c, a0c, g2)


def _prep_kernel(up_ref, mid_ref, dn_ref, ics_ref, cw_ref, kk_w_ref, ka_ref, rk_ref, seg_ref,
                 r_ref, v_ref, kk_ref, k_ref, bonus_ref, *, width, vertical, tiles_per_image):
    tm = mid_ref.shape[0]
    cw = cw_ref[...]
    mid = mid_ref[...]
    xpos = lax.broadcasted_iota(jnp.int32, (tm, 1), 0) % width
    not_first = xpos != 0
    not_last = xpos != width - 1

    rows = [(mid, 1)]
    if vertical:
        t = pl.program_id(0) % tiles_per_image
        up = jnp.where(t == 0, 0.0, up_ref[...])
        dn = jnp.where(t == tiles_per_image - 1, 0.0, dn_ref[...])
        ext = jnp.concatenate([up, mid, dn], axis=0)
        rows += [(ext[0:tm], 0), (ext[2 * width:2 * width + tm], 2)]

    def column(dx):
        return sum(base * cw[3 * dy + dx:3 * dy + dx + 1] for base, dy in rows)

    acc = (column(1) + jnp.where(not_first, pltpu.roll(column(0), 1, 0), 0.0)
           + jnp.where(not_last, pltpu.roll(column(2), tm - 1, 0), 0.0))

    w = RWKV_WIDTH
    r, k, v = acc[:, :w], acc[:, w:2 * w], acc[:, 2 * w:]
    seg = seg_ref[...]
    kk = k * kk_w_ref[...]
    kk = kk * lax.rsqrt(_dot_right01(kk * kk, seg) + 1e-12)
    r_ref[...] = r
    v_ref[...] = v
    kk_ref[...] = kk
    k_ref[...] = k
    k_sum = k * (2.0 + (ics_ref[...] - 2.0) * ka_ref[...])
    bonus_ref[...] = _dot_right01(r * rk_ref[...] * k_sum, seg) * v


def _prep_call(rkv, ics, cw, kk_w, ka, rk, seg, *, tokens_per_batch, width, vertical):
    n = rkv.shape[0]
    tm = min(TOKEN_TILE, tokens_per_batch)
    rows_per_tile = tm // width
    n_rows = n // width
    w3 = 3 * RWKV_WIDTH
    w = RWKV_WIDTH
    tok = pl.BlockSpec((tm, w), lambda i: (i, 0))
    return pl.pallas_call(
        functools.partial(_prep_kernel, width=width, vertical=vertical, tiles_per_image=tokens_per_batch // tm),
        grid=(n // tm,),
        in_specs=[pl.BlockSpec((width, w3), lambda i: (jnp.maximum(i * rows_per_tile - 1, 0), 0)),
                  pl.BlockSpec((tm, w3), lambda i: (i, 0)),
                  pl.BlockSpec((width, w3), lambda i: (jnp.minimum((i + 1) * rows_per_tile, n_rows - 1), 0)),
                  tok,
                  _resident((9, w3)), _resident((1, w)), _resident((1, w)), _resident((1, w)), _resident((w, w))],
        out_specs=[tok] * 5,
        out_shape=[jax.ShapeDtypeStruct((n, w), F32)] * 5,
        compiler_params=_cparams("parallel"),
        name="prep",
    )(rkv, rkv, rkv, ics, cw, kk_w, ka, rk, seg)


def _s5_kernel(uc_ref, ux_ref, wcat_ref, tfb_ref, cp_ref, ab_ref, y_ref, x_scr, h_scr, *, ctx_chunks, x_chunks, nb):
    ux = ux_ref[0]
    r0 = ctx_chunks * nb
    x_scr[:r0] = _dot(uc_ref[0], wcat_ref[0])
    x_scr[r0:] = _dot(ux, wcat_ref[0])
    ab = ab_ref[0]
    total = ctx_chunks + x_chunks

    def body(k, carry):
        hf, hfs, hb, hbs = carry
        cb = jnp.where(k < ctx_chunks, ctx_chunks - 1 - k, total + ctx_chunks - 1 - k)
        off_f = pl.multiple_of(k * nb, nb)
        off_b = pl.multiple_of(cb * nb, nb)
        h_scr[pl.ds(off_f, nb), 0:LANE] = hf
        h_scr[pl.ds(off_b, nb), LANE:2 * LANE] = hb
        xf, xfs = x_scr[pl.ds(off_f, nb), 0:LANE], x_scr[pl.ds(off_f, nb), LANE:2 * LANE]
        xb, xbs = x_scr[pl.ds(off_b, nb), 2 * LANE:3 * LANE], x_scr[pl.ds(off_b, nb), 3 * LANE:4 * LANE]
        return (ab[0] * hf + ab[1] * hfs + xf, ab[0] * hfs - ab[1] * hf + xfs,
                ab[2] * hb + ab[3] * hbs + xb, ab[2] * hbs - ab[3] * hb + xbs)

    zero = jnp.zeros((nb, LANE), F32)
    lax.fori_loop(0, total, body, (zero,) * 4, unroll=S5_SCAN_UNROLL)

    y_ref[0] = _dot(ux, tfb_ref[0]) + _dot(h_scr[r0:, :].astype(BF16), cp_ref[0])


def _s5_call(u_c, u_x, wcat, tfb, cp, ab, *, nb):
    g = u_x.shape[0]
    ctx_chunks = u_c.shape[1] // nb
    x_chunks = u_x.shape[1] // nb
    rows = (ctx_chunks + x_chunks) * nb
    lw = S5_CHUNK * S5_GROUP
    return pl.pallas_call(
        functools.partial(_s5_kernel, ctx_chunks=ctx_chunks, x_chunks=x_chunks, nb=nb),
        grid=(g,),
        in_specs=[pl.BlockSpec((1, ctx_chunks * nb, lw), lambda i: (i, 0, 0)),
                  pl.BlockSpec((1, x_chunks * nb, lw), lambda i: (i, 0, 0)),
                  pl.BlockSpec((1, lw, 512), lambda i: (i, 0, 0)),
                  pl.BlockSpec((1, lw, lw), lambda i: (i, 0, 0)),
                  pl.BlockSpec((1, 256, lw), lambda i: (i, 0, 0)),
                  pl.BlockSpec((1, 4, nb, 128), lambda i: (i, 0, 0, 0))],
        out_specs=pl.BlockSpec((1, x_chunks * nb, lw), lambda i: (i, 0, 0)),
        out_shape=jax.ShapeDtypeStruct((g, x_chunks * nb, lw), F32),
        scratch_shapes=[pltpu.VMEM((rows, 512), F32), pltpu.VMEM((rows, 256), F32)],
        compiler_params=_cparams("parallel"),
        name="s5",
    )(u_c, u_x, wcat, tfb, cp, ab)


def _s5_matrices(lam_re, lam_im, log_dt, b_re, b_im, c_re, c_im, nb):
    L, P, GS = S5_CHUNK, S5_STATE, S5_GROUP
    hp = lax.Precision.HIGHEST
    dt = jnp.exp(log_dt)[..., None]
    ab_re = jnp.exp(dt * lam_re) * jnp.cos(dt * lam_im)
    ab_im = jnp.exp(dt * lam_re) * jnp.sin(dt * lam_im)
    den = lam_re * lam_re + lam_im * lam_im
    z_re = ((ab_re - 1.0) * lam_re + ab_im * lam_im) / den
    z_im = (ab_im * lam_re - (ab_re - 1.0) * lam_im) / den
    bb_re = z_re[..., None] * b_re - z_im[..., None] * b_im
    bb_im = z_re[..., None] * b_im + z_im[..., None] * b_re
    k = jnp.arange(L + 1, dtype=F32)[:, None, None, None]
    pw_re = jnp.exp(k * dt * lam_re) * jnp.cos(k * dt * lam_im)
    pw_im = jnp.exp(k * dt * lam_re) * jnp.sin(k * dt * lam_im)
    ca_re = c_re[None] * pw_re[:, :, :, None, :] - c_im[None] * pw_im[:, :, :, None, :]
    ca_im = c_re[None] * pw_im[:, :, :, None, :] + c_im[None] * pw_re[:, :, :, None, :]
    ab_re_k = pw_re[..., None] * bb_re[None] - pw_im[..., None] * bb_im[None]
    ab_im_k = pw_re[..., None] * bb_im[None] + pw_im[..., None] * bb_re[None]
    kern = (jnp.einsum('kdgip,dgpj->dgkij', ca_re[:L], bb_re, precision=hp)
            - jnp.einsum('kdgip,dgpj->dgkij', ca_im[:L], bb_im, precision=hp))

    s = jnp.arange(L)[:, None]
    t = jnp.arange(L)[None, :]

    def toeplitz(kd, lag, valid):
        pick = ((lag[:, :, None] == jnp.arange(L)) & valid[:, :, None]).astype(F32)
        return jnp.einsum('stk,gkij->gsjti', pick, kd, precision=hp).reshape(-1, L * GS, L * GS)

    tfb = toeplitz(kern[0], t - s, t >= s) + toeplitz(kern[1], s - t, s >= t)

    def state_in(d, powers):
        wr = ab_re_k[powers, d].astype(BF16).transpose(1, 0, 3, 2).reshape(-1, L * GS, P)
        wi = ab_im_k[powers, d].astype(BF16).transpose(1, 0, 3, 2).reshape(-1, L * GS, P)
        return jnp.concatenate([wr, wi, wi, wr], axis=-1)

    wcat = jnp.concatenate([state_in(0, L - 1 - jnp.arange(L)), state_in(1, jnp.arange(L))], axis=-1)

    def state_out(d, powers):
        cr = ca_re[powers, d].astype(BF16).transpose(1, 3, 0, 2).reshape(-1, P, L * GS)
        ci = ca_im[powers, d].astype(BF16).transpose(1, 3, 0, 2).reshape(-1, P, L * GS)
        return jnp.concatenate([cr, -ci], axis=1)

    cp = jnp.concatenate([state_out(0, 1 + jnp.arange(L)), state_out(1, L - jnp.arange(L))], axis=1)

    def step_coeffs(d):
        a1 = jnp.concatenate([pw_re[L, d], pw_re[L, d]], axis=-1)
        a2 = jnp.concatenate([-pw_im[L, d], pw_im[L, d]], axis=-1)
        return [a1, a2]

    ab = jnp.stack(step_coeffs(0) + step_coeffs(1), axis=1)
    ab = jnp.broadcast_to(ab[:, :, None, :], ab.shape[:2] + (nb, 128))
    return wcat, tfb.astype(BF16), cp, ab


def _rwkv_kernel(*refs, nc):
    pre_f32, pre_bf16, pre_w = refs[-3:]
    s_scr = refs[-4]
    i = pl.program_id(0)

    @pl.when(i == 0)
    def _():
        pre_f32[1] = jnp.zeros(pre_f32.shape[1:], F32)
        pre_bf16[1] = jnp.zeros(pre_bf16.shape[1:], BF16)
        pre_w[1] = jnp.zeros(pre_w.shape[1:], F32)
        s_scr[...] = jnp.zeros(s_scr.shape, F32)

    for parity in range(2):
        @pl.when(i % 2 == parity)
        def _():
            _rwkv_step(*refs, nc=nc, slot_new=parity)


def _rwkv_step(rf_ref, vf_ref, kkf_ref, rb_ref, vb_ref, kkb_ref, kf_ref, kb_ref, lwf_ref, lwb_ref,
               icf_ref, icb_ref, s0_ref, ka_ref, yf_ref, yb_ref, sT_ref, s_scr, pre_f32, pre_bf16, pre_w,
               *, nc, slot_new):
    slot_cur = 1 - slot_new
    c = jnp.maximum(pl.program_id(0) - 1, 0) % nc
    L, N = RWKV_CHUNK, RWKV_HEAD
    P = 2 * N
    n_pairs = RWKV_WIDTH // P

    row = lax.broadcasted_iota(jnp.int32, (L, P), 0)
    lane = lax.broadcasted_iota(jnp.int32, (L, P), 1)
    col = lane % N
    first = lane < N
    eye = (col == row).astype(F32)
    strict = (col < row, col > row)
    incl = (col <= row, col >= row)
    row1 = lax.broadcasted_iota(jnp.int32, (L, L), 0)
    col1 = lax.broadcasted_iota(jnp.int32, (L, L), 1)
    tri = (col1 <= row1, col1 >= row1)

    def bd(y):
        return jnp.concatenate([jnp.where(first, y, 0).astype(BF16), jnp.where(first, 0, y).astype(BF16)], axis=0)

    def pmm(x, y):
        return _dot(x.astype(BF16), bd(y))

    def pmm_tn(x, y):
        full = _dot(x.T.astype(BF16), y.astype(BF16))
        return jnp.where(first, full[:N], full[N:])

    bt = rf_ref.shape[0]
    per_dir = ((rf_ref, vf_ref, kkf_ref, kf_ref, lwf_ref, icf_ref), (rb_ref, vb_ref, kkb_ref, kb_ref, lwb_ref, icb_ref))
    for bl in range(bt):
        for d, (r_ref, v_ref, kk_ref, k_ref, lw_ref, ic_ref) in enumerate(per_dir):
            lw = lw_ref[bl]
            cum = _dot_left01(jnp.where(tri[d], 1.0, 0.0).astype(BF16), lw)
            tot = jnp.sum(lw, axis=0, keepdims=True)
            kk = kk_ref[bl]
            ic = ic_ref[bl]
            b = kk * ic
            k = k_ref[bl] * (1.0 + (ic - 1.0) * ka_ref[...])
            e_ninc = jnp.exp(-cum)
            e_rem = jnp.exp(tot - cum)
            pre_f32[slot_new, bl, d, 0] = -kk * jnp.exp(cum - lw)
            pre_f32[slot_new, bl, d, 1] = r_ref[bl] * jnp.exp(cum)
            pre_f32[slot_new, bl, d, 2] = v_ref[bl]
            pre_bf16[slot_new, bl, d, 0] = (b * e_ninc).astype(BF16)
            pre_bf16[slot_new, bl, d, 1] = (k * e_ninc).astype(BF16)
            pre_bf16[slot_new, bl, d, 2] = (b * e_rem).astype(BF16)
            pre_bf16[slot_new, bl, d, 3] = (k * e_rem).astype(BF16)
            pre_w[slot_new, bl, d] = jnp.exp(tot)

    chains = [(bl, d, p) for bl in range(bt) for d in range(2) for p in range(n_pairs)]

    def pair(scr, j):
        return [scr[slot_cur, bl, d, j, :, p * P:(p + 1) * P] for bl, d, p in chains]

    a_tp, r_tp, vp = (pair(pre_f32, j) for j in range(3))
    b_tp, k_tp, b_hp, k_hp = (pair(pre_bf16, j) for j in range(4))
    w_p = [pre_w[slot_cur, bl, d, :, p * P:(p + 1) * P] for bl, d, p in chains]
    def rows2(x, y):
        return jnp.concatenate([x, y], axis=0)

    def cols2(x, y):
        return jnp.concatenate([x, y], axis=1)

    ar = [rows2(a, r).astype(BF16) for a, r in zip(a_tp, r_tp)]
    gram = [_dot_nt(x, rows2(bd(y), bd(z))) for x, y, z in zip(ar, b_tp, k_tp)]
    n_ab = [jnp.where(strict[d], g[:L, :P], 0.0) for (_, d, _), g in zip(chains, gram)]
    a_ak = [jnp.where(strict[d], g[:L, P:], 0.0) for (_, d, _), g in zip(chains, gram)]
    a_rb = [jnp.where(incl[d], g[L:, :P], 0.0).astype(BF16) for (_, d, _), g in zip(chains, gram)]
    a_rk = [jnp.where(incl[d], g[L:, P:], 0.0) for (_, d, _), g in zip(chains, gram)]
    kv = [pmm(rows2(x, y), z) for x, y, z in zip(a_ak, a_rk, vp)]
    inv = [eye + x for x in n_ab]
    pw = [pmm(x, x) for x in n_ab]
    for _ in range(4):
        both = [pmm(rows2(x, t), x) for x, t in zip(pw, inv)]
        inv = [t + z[L:] for t, z in zip(inv, both)]
        pw = [z[:L] for z in both]
    inv = [t + pmm(t, x) for t, x in zip(inv, pw)]
    invb = [x.astype(BF16) for x in inv]
    apvp = [_dot(t, cols2(bd(a), bd(z[:L]))) for t, a, z in zip(invb, a_tp, kv)]
    a_p = [z[:, :P] for z in apvp]
    v_p = [z[:, P:] for z in apvp]
    rb = [_dot(x, cols2(bd(y), bd(z))) for x, y, z in zip(a_rb, a_p, v_p)]
    r_p = [r + z[:, :P] for r, z in zip(r_tp, rb)]
    y_loc = [z[:, P:] + w[L:] for z, w in zip(rb, kv)]
    tn = [_dot(z.T.astype(BF16), y) for z, y in zip(apvp, b_hp)]
    p_lr = [jnp.where(first, z[:N], z[N:P]) for z in tn]
    q = [jnp.where(first, z[P:P + N], z[P + N:]) + pmm_tn(x, y) for z, x, y in zip(tn, vp, k_hp)]

    s = [jnp.where(c == 0, s0_ref[d, bl, p], s_scr[bl, d, p]) for bl, d, p in chains]
    s_bd = [bd(x) for x in s]
    y = [_dot_nt(x.astype(BF16), z) + yl for x, z, yl in zip(r_p, s_bd, y_loc)]
    s_new = [x * w + pmm(x, pl_) + qq for x, w, pl_, qq in zip(s, w_p, p_lr, q)]
    for (bl, d, p), yy, ss in zip(chains, y, s_new):
        (yf_ref, yb_ref)[d][bl, :, p * P:(p + 1) * P] = yy
        s_scr[bl, d, p] = ss
        sT_ref[d, bl, p] = ss


def _rwkv_call(r, v, kk, k, lw2, ic2, s0, ka):
    nb, n, w = r.shape
    bt = RWKV_BATCH
    nc = n // RWKV_CHUNK
    items = nb // bt * nc

    def fwd_chunk(item):
        return item % nc

    def bwd_chunk(item):
        return nc - 1 - item % nc

    def item_in(i):
        return jnp.minimum(i, items - 1)

    def item_out(i):
        return jnp.maximum(i - 1, 0)

    def tok(chunk, item):
        return pl.BlockSpec((bt, RWKV_CHUNK, w), lambda i: (item(i) // nc, chunk(item(i)), 0))

    both = [pl.BlockSpec((None, bt, RWKV_CHUNK, w), lambda i: (0, item_in(i) // nc, fwd_chunk(item_in(i)), 0)),
            pl.BlockSpec((None, bt, RWKV_CHUNK, w), lambda i: (1, item_in(i) // nc, bwd_chunk(item_in(i)), 0))]
    st = pl.BlockSpec((2, bt) + RWKV_STATE, lambda i: (0, item_out(i) // nc, 0, 0, 0))
    fwd_in, bwd_in = tok(fwd_chunk, item_in), tok(bwd_chunk, item_in)
    out = jax.ShapeDtypeStruct((nb, n, w), F32)
    return pl.pallas_call(
        functools.partial(_rwkv_kernel, nc=nc),
        grid=(items + 1,),
        in_specs=[fwd_in, fwd_in, fwd_in, bwd_in, bwd_in, bwd_in, fwd_in, bwd_in, *both, *both, st,
                  _resident((1, w))],
        out_specs=[tok(fwd_chunk, item_out), tok(bwd_chunk, item_out), st],
        out_shape=[out, out, jax.ShapeDtypeStruct((2, nb) + RWKV_STATE, F32)],
        scratch_shapes=[pltpu.VMEM((bt, 2) + RWKV_STATE, F32),
                        pltpu.VMEM((2, bt, 2, 3, RWKV_CHUNK, w), F32),
                        pltpu.VMEM((2, bt, 2, 4, RWKV_CHUNK, w), BF16),
                        pltpu.VMEM((2, bt, 2, 1, w), F32)],
        compiler_params=_cparams("arbitrary"),
        name="rwkv",
    )(r, v, kk, r, v, kk, k, k, lw2, lw2, ic2, ic2, s0, ka)


def _out_kernel(x_ref, mod_ref, ya_ref, u_ref, ybf_ref, ybb_ref, bonus_ref, gg_ref, sga_ref, sgb_ref,
                s5d_ref, wglu_ref, wproj_ref, lng_ref, lnb_ref, seg_ref, wo_ref, wout_ref, o_ref, ya_scr):
    nb, tn, _ = x_ref.shape

    per_tile = LANE // S5_GROUP
    for cl in range(tn // S5_CHUNK):
        for q in range(S5_WIDTH // LANE):
            per_g = [ya_ref[q * per_tile + gq, cl * nb:(cl + 1) * nb, :] for gq in range(per_tile)]
            for t in range(S5_CHUNK):
                sl = slice(t * S5_GROUP, (t + 1) * S5_GROUP)
                ya_scr[q, pl.ds(cl * S5_CHUNK + t, nb, stride=tn), :] = jnp.concatenate(
                    [y[:, sl] for y in per_g], axis=1)
    ya_tok = jnp.concatenate([ya_scr[q] for q in range(S5_WIDTH // LANE)], axis=1)
    rows = nb * tn

    def get(ref):
        return ref[...].reshape(rows, ref.shape[-1]).astype(F32)

    ya = jax.nn.gelu(ya_tok + s5d_ref[...] * get(u_ref))
    ya = ya * jax.nn.sigmoid(_dot(ya.astype(BF16), wglu_ref[...]))
    pa = _dot(ya.astype(BF16), wproj_ref[...])

    seg = seg_ref[...]
    yb = get(ybf_ref) + get(ybb_ref)
    mu = _dot_right01(yb, seg) * (1.0 / RWKV_HEAD)
    cen = yb - mu
    var = _dot_right01(cen * cen, seg) * (1.0 / RWKV_HEAD)
    yb = cen * lax.rsqrt(var + LN_X_EPS) * lng_ref[...] + lnb_ref[...] + get(bonus_ref)
    pb = _dot((yb * get(gg_ref)).astype(BF16), wo_ref[...])

    merged = get(sga_ref) * pa + get(sgb_ref) * pb
    out = _dot(merged.astype(BF16), wout_ref[...]).reshape(nb, tn, D_MODEL)
    o_ref[...] = x_ref[...] + mod_ref[:, 5:6] * out


def _out_call(x, mods, ya, u, ybf, ybb, bonus, gg, sga, sgb, s5d, wglu, wproj, lng, lnb, seg, wo, wout):
    nb, n, _ = x.shape
    tn = BATCH_TILE
    w = RWKV_WIDTH

    def tok(width):
        return pl.BlockSpec((nb, tn, width), lambda i: (0, i, 0))

    s5_rows = tn // S5_CHUNK * nb
    return pl.pallas_call(
        _out_kernel,
        grid=(n // tn,),
        in_specs=[tok(D_MODEL), _resident((nb, N_MOD, D_MODEL)),
                  pl.BlockSpec((S5_GROUPS, s5_rows, S5_CHUNK * S5_GROUP), lambda i: (0, i, 0)), tok(w),
                  tok(w), tok(w), tok(w), tok(w), tok(D_MODEL), tok(D_MODEL),
                  _resident((1, w)), _resident((w, w)), _resident((w, D_MODEL)), _resident((1, w)),
                  _resident((1, w)), _resident((w, w)), _resident((w, D_MODEL)), _resident((D_MODEL, D_MODEL))],
        out_specs=tok(D_MODEL),
        out_shape=jax.ShapeDtypeStruct((nb, n, D_MODEL), F32),
        scratch_shapes=[pltpu.VMEM((S5_WIDTH // LANE, nb * tn, LANE), F32)],
        compiler_params=_cparams("parallel"),
        name="outk",
    )(x, mods, ya, u, ybf, ybb, bonus, gg, sga, sgb, s5d, wglu, wproj, lng, lnb, seg, wo, wout)


def _block_diag2(a):
    z = jnp.zeros_like(a[0])
    return jnp.concatenate([jnp.concatenate([a[0], z], axis=1), jnp.concatenate([z, a[1]], axis=1)], axis=0)


def kernel(x, c, ctx, c_ctx, w_mod, b_mod, norm_g, ffn_w_gate, ffn_w_up, ffn_w_down, w_in, s5_A_re, s5_A_im, s5_log_dt, s5_B_re, s5_B_im, s5_C_re, s5_C_im, s5_D, s5_w_glu, s5_w_proj, rwkv_conv, rwkv_w0, rwkv_w2, rwkv_a0, rwkv_a2, rwkv_g2, rwkv_k_k, rwkv_k_a, rwkv_r_k, rwkv_ln_g, rwkv_ln_b, rwkv_w_o, w_out, final_g):
    nb, seq, d = x.shape
    n_ctx = ctx.shape[1]
    l = 0
    w = RWKV_WIDTH

    mod_rows = 16
    cc = jnp.concatenate([c, c_ctx[None], jnp.zeros((mod_rows - nb - 1, d), F32)], axis=0)
    mods = _mod_call(cc, w_mod[l], b_mod[l][None]).reshape(mod_rows, N_MOD, d)

    xt = x.reshape(nb * seq, d)
    ct = ctx.reshape(nb * n_ctx, d)
    row = lambda a: a.reshape(1, -1)
    bf = lambda a: a.astype(BF16)

    f1 = (row(norm_g[l, 0]), bf(ffn_w_gate[l, 0]), bf(ffn_w_up[l, 0]), bf(ffn_w_down[l, 0]), row(final_g))
    x1 = _ffn_call(xt, mods, *f1, j=0, tokens_per_batch=seq, shared_row=None, final_norm=False)
    c1 = _ffn_call(ct, mods, *f1, j=0, tokens_per_batch=n_ctx, shared_row=nb, final_norm=False)

    ip = (row(norm_g[l, 1]), bf(w_in[l]), bf(_block_diag2(rwkv_w2[l])), row(rwkv_w0[l]),
          bf(_block_diag2(rwkv_a2[l])), row(rwkv_a0[l]), bf(rwkv_g2[l]))
    mods_x = mods[:nb]
    mods_c = jnp.broadcast_to(mods[nb:nb + 1], (nb, N_MOD, d))
    u_x, us5_x, rkv_x, lw_x, ic_x, ics_x, gg_x, sga_x, sgb_x = _inproj_call(x1.reshape(nb, seq, d), mods_x, *ip)
    _, us5_c, rkv_c, lw_c, ic_c, ics_c, _, _, _ = _inproj_call(c1.reshape(nb, n_ctx, d), mods_c, *ip)

    def flat(a):
        return a.reshape(a.shape[:-3] + (a.shape[-3] * a.shape[-2], a.shape[-1]))

    s5p = _s5_matrices(s5_A_re[l], s5_A_im[l], s5_log_dt[l], s5_B_re[l], s5_B_im[l], s5_C_re[l], s5_C_im[l], nb)
    ya_s5 = _s5_call(us5_c, us5_x, *s5p, nb=nb)

    seg = (jnp.arange(w)[:, None] // RWKV_HEAD == jnp.arange(w)[None, :] // RWKV_HEAD).astype(BF16)
    pp = (rwkv_conv[l].reshape(9, 3 * w), row(rwkv_k_k[l]), row(rwkv_k_a[l]), row(rwkv_r_k[l]), seg)
    r_c, v_c, kk_c, k_c, _ = _prep_call(flat(rkv_c), flat(ics_c), *pp, tokens_per_batch=n_ctx, width=n_ctx,
                                        vertical=False)
    r_x, v_x, kk_x, k_x, bonus = _prep_call(flat(rkv_x), flat(ics_x), *pp, tokens_per_batch=seq, width=GRID_W,
                                            vertical=True)

    def by_batch(a, n):
        return a.reshape(nb, n, a.shape[-1])

    s0 = jnp.zeros((2, nb) + RWKV_STATE, F32)
    ka = row(rwkv_k_a[l])
    _, _, s_ctx = _rwkv_call(*(by_batch(a, n_ctx) for a in (r_c, v_c, kk_c, k_c)), lw_c, ic_c, s0, ka)
    ybf, ybb, _ = _rwkv_call(*(by_batch(a, seq) for a in (r_x, v_x, kk_x, k_x)), lw_x, ic_x, s_ctx, ka)

    x2 = _out_call(by_batch(x1, seq), mods_x, ya_s5, u_x, ybf, ybb, by_batch(bonus, seq), gg_x, sga_x,
                   sgb_x, row(s5_D[l]), bf(s5_w_glu[l]), bf(s5_w_proj[l]), row(rwkv_ln_g[l]), row(rwkv_ln_b[l]),
                   seg, bf(rwkv_w_o[l]), bf(w_out[l]))

    f2 = (row(norm_g[l, 2]), bf(ffn_w_gate[l, 1]), bf(ffn_w_up[l, 1]), bf(ffn_w_down[l, 1]), row(final_g))
    out = _ffn_call(flat(x2), mods, *f2, j=2, tokens_per_batch=seq, shared_row=None, final_norm=True)
    return out.reshape(nb, seq, d)
```

```python
import functools
import math

import jax
import jax.numpy as jnp
from jax import lax
from jax.experimental import pallas as pl
from jax.experimental.pallas import tpu as pltpu

F32 = jnp.float32
BF16 = jnp.bfloat16

D_MODEL = 1024
N_MOD = 9
FFN_DIM = 2816
RMS_EPS = 1e-6
GRID_W = 64
S5_WIDTH = 512
S5_GROUP = 16
S5_GROUPS = 32
S5_STATE = 64
RWKV_WIDTH = 512
RWKV_HEAD = 64
RWKV_HEADS = 8
LN_X_EPS = 64e-5
DECAY_LORA = 64
AAA_LORA = 64
GATE_LORA = 128
LANE = 128

S5_CHUNK = 16
RWKV_CHUNK = 64
TOKEN_TILE = 512
BATCH_TILE = 64
S5_SCAN_UNROLL = 4
ROW_SPLIT = 2
RWKV_STATE = (RWKV_HEADS // 2, RWKV_HEAD, 2 * RWKV_HEAD)
RWKV_BATCH = 4
VMEM_LIMIT = 56 * 1024 * 1024


def _cparams(*sem):
    return pltpu.CompilerParams(dimension_semantics=sem, vmem_limit_bytes=VMEM_LIMIT)


def _resident(shape):
    nd = len(shape)
    return pl.BlockSpec(shape, lambda *_: (0,) * nd, pipeline_mode=pl.Buffered(1))


def _dot(a, b):
    return jnp.dot(a, b, preferred_element_type=F32)


def _dot_nt(a, b):
    return lax.dot_general(a, b, (((1,), (1,)), ((), ())), preferred_element_type=F32)


def _split2(x):
    hi = x.astype(BF16)
    return hi, (x - hi.astype(F32)).astype(BF16)


def _dot_left01(m01, x):
    hi, lo = _split2(x)
    return _dot(m01, hi) + _dot(m01, lo)


def _dot_right01(x, m01):
    hi, lo = _split2(x)
    return _dot(hi, m01) + _dot(lo, m01)


def _rms_mod(x, g, shift, scale):
    y = x * lax.rsqrt(jnp.mean(x * x, axis=-1, keepdims=True) + RMS_EPS)
    return (y * g) * (1.0 + scale) + shift


def _mod_kernel(c_ref, w_ref, b_ref, o_ref):
    c = c_ref[...]
    s = c * jax.nn.sigmoid(c)
    o_ref[...] = jnp.dot(s, w_ref[...], precision=lax.Precision.HIGHEST, preferred_element_type=F32) + b_ref[...]


def _mod_call(cc, w_mod, b_mod):
    rows = cc.shape[0]
    return pl.pallas_call(
        _mod_kernel,
        grid=(N_MOD,),
        in_specs=[pl.BlockSpec((rows, D_MODEL), lambda j: (0, 0)),
                  pl.BlockSpec((D_MODEL, D_MODEL), lambda j: (0, j)),
                  pl.BlockSpec((1, D_MODEL), lambda j: (0, j))],
        out_specs=pl.BlockSpec((rows, D_MODEL), lambda j: (0, j)),
        out_shape=jax.ShapeDtypeStruct((rows, N_MOD * D_MODEL), F32),
        compiler_params=_cparams("arbitrary"),
        name="mod",
    )(cc, w_mod, b_mod)


def _ffn_kernel(x_ref, mod_ref, g_ref, wg_ref, wu_ref, wd_ref, fg_ref, o_ref, *, j, final_norm):
    x = x_ref[...]
    m = mod_ref[0]
    h = _rms_mod(x, g_ref[...], m[3 * j:3 * j + 1], m[3 * j + 1:3 * j + 2]).astype(BF16)
    a = _dot(h, wg_ref[...])
    a = (a * jax.nn.sigmoid(a)) * _dot(h, wu_ref[...])
    y = _dot(a.astype(BF16), wd_ref[...])
    out = x + 0.5 * m[3 * j + 2:3 * j + 3] * y
    if final_norm:
        out = out * lax.rsqrt(jnp.mean(out * out, axis=-1, keepdims=True) + RMS_EPS) * fg_ref[...]
    o_ref[...] = out


def _mod_spec(n_tiles, tiles_per_batch, shared_row):
    if shared_row is not None:
        return pl.BlockSpec((1, N_MOD, D_MODEL), lambda i: (shared_row, 0, 0))
    assert n_tiles % tiles_per_batch == 0
    return pl.BlockSpec((1, N_MOD, D_MODEL), lambda i: (i // tiles_per_batch, 0, 0))


def _ffn_call(x, mods, g, wg, wu, wd, fg, *, j, tokens_per_batch, shared_row, final_norm):
    n = x.shape[0]
    tm = TOKEN_TILE if shared_row is not None else min(TOKEN_TILE, tokens_per_batch)
    tok = pl.BlockSpec((tm, D_MODEL), lambda i: (i, 0))
    return pl.pallas_call(
        functools.partial(_ffn_kernel, j=j, final_norm=final_norm),
        grid=(n // tm,),
        in_specs=[tok, _mod_spec(n // tm, tokens_per_batch // tm, shared_row), _resident((1, D_MODEL)),
                  _resident((D_MODEL, FFN_DIM)), _resident((D_MODEL, FFN_DIM)), _resident((FFN_DIM, D_MODEL)),
                  _resident((1, D_MODEL))],
        out_specs=tok,
        out_shape=jax.ShapeDtypeStruct((n, D_MODEL), F32),
        compiler_params=_cparams("parallel"),
        name="ffn",
    )(x, mods, g, wg, wu, wd, fg)


O_U = S5_WIDTH
O_RKV = O_U + 3 * RWKV_WIDTH
O_WD = O_RKV + 2 * DECAY_LORA
O_AD = O_WD + 2 * AAA_LORA
O_GD = O_AD + GATE_LORA
O_GA = O_GD + D_MODEL
IN_COLS = O_GA + D_MODEL
DECAY_SCALE = math.exp(-0.5)


def _inproj_kernel(x_ref, mod_ref, g_ref, w_ref, w2_ref, w0_ref, a2_ref, a0_ref, g2_ref,
                   u_ref, us5_ref, rkv_ref, lw_ref, ic_ref, ics_ref, gg_ref, sga_ref, sgb_ref, u_scr):
    nb, tn, _ = x_ref.shape
    hb = nb // ROW_SPLIT
    for part in range(ROW_SPLIT):
        bs = slice(part * hb, (part + 1) * hb)
        rows = hb * tn
        m = mod_ref[bs]
        h = _rms_mod(x_ref[bs], g_ref[...], m[:, 3:4], m[:, 4:5]).reshape(rows, D_MODEL).astype(BF16)
        proj = _dot(h, w_ref[...])

        def put(ref, val):
            ref[bs] = val.reshape(hb, tn, val.shape[-1]).astype(ref.dtype)

        u = proj[:, :O_U]
        put(u_ref, u)
        put(rkv_ref, proj[:, O_U:O_RKV])
        w_log = w0_ref[...] + _dot(jnp.tanh(proj[:, O_RKV:O_WD]).astype(BF16), w2_ref[...])
        lw = -DECAY_SCALE * jax.nn.sigmoid(w_log)
        ic = jax.nn.sigmoid(a0_ref[...] + _dot(proj[:, O_WD:O_AD].astype(BF16), a2_ref[...]))
        for d in range(2):
            lw_ref[d, bs] = lw[:, d * RWKV_WIDTH:(d + 1) * RWKV_WIDTH].reshape(hb, tn, RWKV_WIDTH)
            ic_ref[d, bs] = ic[:, d * RWKV_WIDTH:(d + 1) * RWKV_WIDTH].reshape(hb, tn, RWKV_WIDTH)
        put(ics_ref, ic[:, :RWKV_WIDTH] + ic[:, RWKV_WIDTH:])
        put(gg_ref, _dot(jax.nn.sigmoid(proj[:, O_AD:O_GD]).astype(BF16), g2_ref[...]))
        put(sga_ref, jax.nn.sigmoid(proj[:, O_GD:O_GA]))
        put(sgb_ref, jax.nn.sigmoid(proj[:, O_GA:]))
        for q in range(S5_WIDTH // LANE):
            u_scr[q, part * rows:(part + 1) * rows] = u[:, q * LANE:(q + 1) * LANE]

    per_tile = LANE // S5_GROUP
    for q in range(S5_WIDTH // LANE):
        taps = [[u_scr[q, pl.ds(cl * S5_CHUNK + t, nb, stride=tn), :] for t in range(S5_CHUNK)]
                for cl in range(tn // S5_CHUNK)]
        for gq in range(per_tile):
            sl = slice(gq * S5_GROUP, (gq + 1) * S5_GROUP)
            rows_g = [jnp.concatenate([a[:, sl] for a in chunk_taps], axis=1) for chunk_taps in taps]
            us5_ref[q * per_tile + gq] = jnp.concatenate(rows_g, axis=0).astype(BF16)


def _inproj_call(x, mods, g, w_in, w2c, w0c, a2c, a0c, g2):
    nb, n, _ = x.shape
    tn = BATCH_TILE
    w = RWKV_WIDTH
    lw5 = S5_CHUNK * S5_GROUP

    def tok(width):
        return pl.BlockSpec((nb, tn, width), lambda i: (0, i, 0))

    tok2 = pl.BlockSpec((2, nb, tn, w), lambda i: (0, 0, i, 0))

    def shp(width, dtype=F32):
        return jax.ShapeDtypeStruct((nb, n, width), dtype)

    shp2 = jax.ShapeDtypeStruct((2, nb, n, w), F32)
    s5_rows = tn // S5_CHUNK * nb
    return pl.pallas_call(
        _inproj_kernel,
        grid=(n // tn,),
        in_specs=[tok(D_MODEL), _resident((nb, N_MOD, D_MODEL)), _resident((1, D_MODEL)),
                  _resident((D_MODEL, IN_COLS)), _resident((2 * DECAY_LORA, 2 * w)), _resident((1, 2 * w)),
                  _resident((2 * AAA_LORA, 2 * w)), _resident((1, 2 * w)), _resident((GATE_LORA, w))],
        out_specs=[tok(S5_WIDTH), pl.BlockSpec((S5_GROUPS, s5_rows, lw5), lambda i: (0, i, 0)), tok(3 * w),
                   tok2, tok2, tok(w), tok(w), tok(D_MODEL), tok(D_MODEL)],
        out_shape=[shp(S5_WIDTH), jax.ShapeDtypeStruct((S5_GROUPS, n // S5_CHUNK * nb, lw5), BF16), shp(3 * w),
                   shp2, shp2, shp(w), shp(w), shp(D_MODEL, BF16), shp(D_MODEL, BF16)],
        scratch_shapes=[pltpu.VMEM((S5_WIDTH // LANE, nb * tn, LANE), F32)],
        compiler_params=_cparams("parallel"),
        name="inproj",
    )(x, mods, g, w_in, w2c, w0c, a2c, a0c, g2)


def _prep_kernel(up_ref, mid_ref, dn_ref, ics_ref, cw_ref, kk_w_ref, ka_ref, rk_ref, seg_ref,
                 r_ref, v_ref, kk_ref, k_ref, bonus_ref, *, width, vertical, tiles_per_image):
    tm = mid_ref.shape[0]
    cw = cw_ref[...]
    mid = mid_ref[...]
    xpos = lax.broadcasted_iota(jnp.int32, (tm, 1), 0) % width
    not_first = xpos != 0
    not_last = xpos != width - 1

    rows = [(mid, 1)]
    if vertical:
        t = pl.program_id(0) % tiles_per_image
        up = jnp.where(t == 0, 0.0, up_ref[...])
        dn = jnp.where(t == tiles_per_image - 1, 0.0, dn_ref[...])
        ext = jnp.concatenate([up, mid, dn], axis=0)
        rows += [(ext[0:tm], 0), (ext[2 * width:2 * width + tm], 2)]

    def column(dx):
        return sum(base * cw[3 * dy + dx:3 * dy + dx + 1] for base, dy in rows)

    acc = (column(1) + jnp.where(not_first, pltpu.roll(column(0), 1, 0), 0.0)
           + jnp.where(not_last, pltpu.roll(column(2), tm - 1, 0), 0.0))

    w = RWKV_WIDTH
    r, k, v = acc[:, :w], acc[:, w:2 * w], acc[:, 2 * w:]
    seg = seg_ref[...]
    kk = k * kk_w_ref[...]
    kk = kk * lax.rsqrt(_dot_right01(kk * kk, seg) + 1e-12)
    r_ref[...] = r
    v_ref[...] = v
    kk_ref[...] = kk
    k_ref[...] = k
    k_sum = k * (2.0 + (ics_ref[...] - 2.0) * ka_ref[...])
    bonus_ref[...] = _dot_right01(r * rk_ref[...] * k_sum, seg) * v


def _prep_call(rkv, ics, cw, kk_w, ka, rk, seg, *, tokens_per_batch, width, vertical):
    n = rkv.shape[0]
    tm = min(TOKEN_TILE, tokens_per_batch)
    rows_per_tile = tm // width
    n_rows = n // width
    w3 = 3 * RWKV_WIDTH
    w = RWKV_WIDTH
    tok = pl.BlockSpec((tm, w), lambda i: (i, 0))
    return pl.pallas_call(
        functools.partial(_prep_kernel, width=width, vertical=vertical, tiles_per_image=tokens_per_batch // tm),
        grid=(n // tm,),
        in_specs=[pl.BlockSpec((width, w3), lambda i: (jnp.maximum(i * rows_per_tile - 1, 0), 0)),
                  pl.BlockSpec((tm, w3), lambda i: (i, 0)),
                  pl.BlockSpec((width, w3), lambda i: (jnp.minimum((i + 1) * rows_per_tile, n_rows - 1), 0)),
                  tok,
                  _resident((9, w3)), _resident((1, w)), _resident((1, w)), _resident((1, w)), _resident((w, w))],
        out_specs=[tok] * 5,
        out_shape=[jax.ShapeDtypeStruct((n, w), F32)] * 5,
        compiler_params=_cparams("parallel"),
        name="prep",
    )(rkv, rkv, rkv, ics, cw, kk_w, ka, rk, seg)


def _s5_kernel(uc_ref, ux_ref, wcat_ref, tfb_ref, cp_ref, ab_ref, y_ref, x_scr, h_scr, *, ctx_chunks, x_chunks, nb):
    ux = ux_ref[0]
    r0 = ctx_chunks * nb
    x_scr[:r0] = _dot(uc_ref[0], wcat_ref[0])
    x_scr[r0:] = _dot(ux, wcat_ref[0])
    ab = ab_ref[0]
    total = ctx_chunks + x_chunks

    def body(k, carry):
        hf, hfs, hb, hbs = carry
        cb = jnp.where(k < ctx_chunks, ctx_chunks - 1 - k, total + ctx_chunks - 1 - k)
        off_f = pl.multiple_of(k * nb, nb)
        off_b = pl.multiple_of(cb * nb, nb)
        h_scr[pl.ds(off_f, nb), 0:LANE] = hf
        h_scr[pl.ds(off_b, nb), LANE:2 * LANE] = hb
        xf, xfs = x_scr[pl.ds(off_f, nb), 0:LANE], x_scr[pl.ds(off_f, nb), LANE:2 * LANE]
        xb, xbs = x_scr[pl.ds(off_b, nb), 2 * LANE:3 * LANE], x_scr[pl.ds(off_b, nb), 3 * LANE:4 * LANE]
        return (ab[0] * hf + ab[1] * hfs + xf, ab[0] * hfs - ab[1] * hf + xfs,
                ab[2] * hb + ab[3] * hbs + xb, ab[2] * hbs - ab[3] * hb + xbs)

    zero = jnp.zeros((nb, LANE), F32)
    lax.fori_loop(0, total, body, (zero,) * 4, unroll=S5_SCAN_UNROLL)

    y_ref[0] = _dot(ux, tfb_ref[0]) + _dot(h_scr[r0:, :].astype(BF16), cp_ref[0])


def _s5_call(u_c, u_x, wcat, tfb, cp, ab, *, nb):
    g = u_x.shape[0]
    ctx_chunks = u_c.shape[1] // nb
    x_chunks = u_x.shape[1] // nb
    rows = (ctx_chunks + x_chunks) * nb
    lw = S5_CHUNK * S5_GROUP
    state = 2 * S5_STATE
    assert state == LANE
    return pl.pallas_call(
        functools.partial(_s5_kernel, ctx_chunks=ctx_chunks, x_chunks=x_chunks, nb=nb),
        grid=(g,),
        in_specs=[pl.BlockSpec((1, ctx_chunks * nb, lw), lambda i: (i, 0, 0)),
                  pl.BlockSpec((1, x_chunks * nb, lw), lambda i: (i, 0, 0)),
                  pl.BlockSpec((1, lw, 4 * state), lambda i: (i, 0, 0)),
                  pl.BlockSpec((1, lw, lw), lambda i: (i, 0, 0)),
                  pl.BlockSpec((1, 2 * state, lw), lambda i: (i, 0, 0)),
                  pl.BlockSpec((1, 4, nb, state), lambda i: (i, 0, 0, 0))],
        out_specs=pl.BlockSpec((1, x_chunks * nb, lw), lambda i: (i, 0, 0)),
        out_shape=jax.ShapeDtypeStruct((g, x_chunks * nb, lw), F32),
        scratch_shapes=[pltpu.VMEM((rows, 4 * state), F32), pltpu.VMEM((rows, 2 * state), F32)],
        compiler_params=_cparams("parallel"),
        name="s5",
    )(u_c, u_x, wcat, tfb, cp, ab)


def _s5_matrices(lam_re, lam_im, log_dt, b_re, b_im, c_re, c_im, nb):
    L, P, GS = S5_CHUNK, S5_STATE, S5_GROUP
    hp = lax.Precision.HIGHEST
    dt = jnp.exp(log_dt)[..., None]
    ab_re = jnp.exp(dt * lam_re) * jnp.cos(dt * lam_im)
    ab_im = jnp.exp(dt * lam_re) * jnp.sin(dt * lam_im)
    den = lam_re * lam_re + lam_im * lam_im
    z_re = ((ab_re - 1.0) * lam_re + ab_im * lam_im) / den
    z_im = (ab_im * lam_re - (ab_re - 1.0) * lam_im) / den
    bb_re = z_re[..., None] * b_re - z_im[..., None] * b_im
    bb_im = z_re[..., None] * b_im + z_im[..., None] * b_re
    k = jnp.arange(L + 1, dtype=F32)[:, None, None, None]
    pw_re = jnp.exp(k * dt * lam_re) * jnp.cos(k * dt * lam_im)
    pw_im = jnp.exp(k * dt * lam_re) * jnp.sin(k * dt * lam_im)
    ca_re = c_re[None] * pw_re[:, :, :, None, :] - c_im[None] * pw_im[:, :, :, None, :]
    ca_im = c_re[None] * pw_im[:, :, :, None, :] + c_im[None] * pw_re[:, :, :, None, :]
    ab_re_k = pw_re[..., None] * bb_re[None] - pw_im[..., None] * bb_im[None]
    ab_im_k = pw_re[..., None] * bb_im[None] + pw_im[..., None] * bb_re[None]
    kern = (jnp.einsum('kdgip,dgpj->dgkij', ca_re[:L], bb_re, precision=hp)
            - jnp.einsum('kdgip,dgpj->dgkij', ca_im[:L], bb_im, precision=hp))

    s = jnp.arange(L)[:, None]
    t = jnp.arange(L)[None, :]

    def toeplitz(kd, lag, valid):
        pick = ((lag[:, :, None] == jnp.arange(L)) & valid[:, :, None]).astype(F32)
        return jnp.einsum('stk,gkij->gsjti', pick, kd, precision=hp).reshape(-1, L * GS, L * GS)

    tfb = toeplitz(kern[0], t - s, t >= s) + toeplitz(kern[1], s - t, s >= t)

    def state_in(d, powers):
        wr = ab_re_k[powers, d].astype(BF16).transpose(1, 0, 3, 2).reshape(-1, L * GS, P)
        wi = ab_im_k[powers, d].astype(BF16).transpose(1, 0, 3, 2).reshape(-1, L * GS, P)
        return jnp.concatenate([wr, wi, wi, wr], axis=-1)

    wcat = jnp.concatenate([state_in(0, L - 1 - jnp.arange(L)), state_in(1, jnp.arange(L))], axis=-1)

    def state_out(d, powers):
        cr = ca_re[powers, d].astype(BF16).transpose(1, 3, 0, 2).reshape(-1, P, L * GS)
        ci = ca_im[powers, d].astype(BF16).transpose(1, 3, 0, 2).reshape(-1, P, L * GS)
        return jnp.concatenate([cr, -ci], axis=1)

    cp = jnp.concatenate([state_out(0, 1 + jnp.arange(L)), state_out(1, L - jnp.arange(L))], axis=1)

    def step_coeffs(d):
        a1 = jnp.concatenate([pw_re[L, d], pw_re[L, d]], axis=-1)
        a2 = jnp.concatenate([-pw_im[L, d], pw_im[L, d]], axis=-1)
        return [a1, a2]

    ab = jnp.stack(step_coeffs(0) + step_coeffs(1), axis=1)
    ab = jnp.broadcast_to(ab[:, :, None, :], ab.shape[:2] + (nb, 2 * P))
    return wcat, tfb.astype(BF16), cp, ab


def _rwkv_kernel(*refs, nc):
    pre_f32, pre_bf16, pre_w = refs[-3:]
    s_scr = refs[-4]
    i = pl.program_id(0)

    @pl.when(i == 0)
    def _():
        pre_f32[1] = jnp.zeros(pre_f32.shape[1:], F32)
        pre_bf16[1] = jnp.zeros(pre_bf16.shape[1:], BF16)
        pre_w[1] = jnp.zeros(pre_w.shape[1:], F32)
        s_scr[...] = jnp.zeros(s_scr.shape, F32)

    for parity in range(2):
        @pl.when(i % 2 == parity)
        def _():
            _rwkv_step(*refs, nc=nc, slot_new=parity)


def _rwkv_step(rf_ref, vf_ref, kkf_ref, rb_ref, vb_ref, kkb_ref, kf_ref, kb_ref, lwf_ref, lwb_ref,
               icf_ref, icb_ref, s0_ref, ka_ref, yf_ref, yb_ref, sT_ref, s_scr, pre_f32, pre_bf16, pre_w,
               *, nc, slot_new):
    slot_cur = 1 - slot_new
    c = jnp.maximum(pl.program_id(0) - 1, 0) % nc
    L, N = RWKV_CHUNK, RWKV_HEAD
    P = 2 * N
    n_pairs = RWKV_WIDTH // P

    row = lax.broadcasted_iota(jnp.int32, (L, P), 0)
    lane = lax.broadcasted_iota(jnp.int32, (L, P), 1)
    col = lane % N
    first = lane < N
    eye = (col == row).astype(F32)
    strict = (col < row, col > row)
    incl = (col <= row, col >= row)
    row1 = lax.broadcasted_iota(jnp.int32, (L, L), 0)
    col1 = lax.broadcasted_iota(jnp.int32, (L, L), 1)
    tri = (col1 <= row1, col1 >= row1)

    def bd(y):
        return jnp.concatenate([jnp.where(first, y, 0).astype(BF16), jnp.where(first, 0, y).astype(BF16)], axis=0)

    def pmm(x, y):
        return _dot(x.astype(BF16), bd(y))

    def pmm_tn(x, y):
        full = _dot(x.T.astype(BF16), y.astype(BF16))
        return jnp.where(first, full[:N], full[N:])

    bt = rf_ref.shape[0]
    per_dir = ((rf_ref, vf_ref, kkf_ref, kf_ref, lwf_ref, icf_ref), (rb_ref, vb_ref, kkb_ref, kb_ref, lwb_ref, icb_ref))
    for bl in range(bt):
        for d, (r_ref, v_ref, kk_ref, k_ref, lw_ref, ic_ref) in enumerate(per_dir):
            lw = lw_ref[bl]
            cum = _dot_left01(jnp.where(tri[d], 1.0, 0.0).astype(BF16), lw)
            tot = jnp.sum(lw, axis=0, keepdims=True)
            kk = kk_ref[bl]
            ic = ic_ref[bl]
            b = kk * ic
            k = k_ref[bl] * (1.0 + (ic - 1.0) * ka_ref[...])
            e_ninc = jnp.exp(-cum)
            e_rem = jnp.exp(tot - cum)
            pre_f32[slot_new, bl, d, 0] = -kk * jnp.exp(cum - lw)
            pre_f32[slot_new, bl, d, 1] = r_ref[bl] * jnp.exp(cum)
            pre_f32[slot_new, bl, d, 2] = v_ref[bl]
            pre_bf16[slot_new, bl, d, 0] = (b * e_ninc).astype(BF16)
            pre_bf16[slot_new, bl, d, 1] = (k * e_ninc).astype(BF16)
            pre_bf16[slot_new, bl, d, 2] = (b * e_rem).astype(BF16)
            pre_bf16[slot_new, bl, d, 3] = (k * e_rem).astype(BF16)
            pre_w[slot_new, bl, d] = jnp.exp(tot)

    chains = [(bl, d, p) for bl in range(bt) for d in range(2) for p in range(n_pairs)]

    def pair(scr, j):
        return [scr[slot_cur, bl, d, j, :, p * P:(p + 1) * P] for bl, d, p in chains]

    a_tp, r_tp, vp = (pair(pre_f32, j) for j in range(3))
    b_tp, k_tp, b_hp, k_hp = (pair(pre_bf16, j) for j in range(4))
    w_p = [pre_w[slot_cur, bl, d, :, p * P:(p + 1) * P] for bl, d, p in chains]
    def rows2(x, y):
        return jnp.concatenate([x, y], axis=0)

    def cols2(x, y):
        return jnp.concatenate([x, y], axis=1)

    ar = [rows2(a, r).astype(BF16) for a, r in zip(a_tp, r_tp)]
    gram = [_dot_nt(x, rows2(bd(y), bd(z))) for x, y, z in zip(ar, b_tp, k_tp)]
    n_ab = [jnp.where(strict[d], g[:L, :P], 0.0) for (_, d, _), g in zip(chains, gram)]
    a_ak = [jnp.where(strict[d], g[:L, P:], 0.0) for (_, d, _), g in zip(chains, gram)]
    a_rb = [jnp.where(incl[d], g[L:, :P], 0.0).astype(BF16) for (_, d, _), g in zip(chains, gram)]
    a_rk = [jnp.where(incl[d], g[L:, P:], 0.0) for (_, d, _), g in zip(chains, gram)]
    kv = [pmm(rows2(x, y), z) for x, y, z in zip(a_ak, a_rk, vp)]
    inv = [eye + x for x in n_ab]
    pw = [pmm(x, x) for x in n_ab]
    for _ in range(4):
        both = [pmm(rows2(x, t), x) for x, t in zip(pw, inv)]
        inv = [t + z[L:] for t, z in zip(inv, both)]
        pw = [z[:L] for z in both]
    inv = [t + pmm(t, x) for t, x in zip(inv, pw)]
    invb = [x.astype(BF16) for x in inv]
    apvp = [_dot(t, cols2(bd(a), bd(z[:L]))) for t, a, z in zip(invb, a_tp, kv)]
    a_p = [z[:, :P] for z in apvp]
    v_p = [z[:, P:] for z in apvp]
    rb = [_dot(x, cols2(bd(y), bd(z))) for x, y, z in zip(a_rb, a_p, v_p)]
    r_p = [r + z[:, :P] for r, z in zip(r_tp, rb)]
    y_loc = [z[:, P:] + w[L:] for z, w in zip(rb, kv)]
    tn = [_dot(z.T.astype(BF16), y) for z, y in zip(apvp, b_hp)]
    p_lr = [jnp.where(first, z[:N], z[N:P]) for z in tn]
    q = [jnp.where(first, z[P:P + N], z[P + N:]) + pmm_tn(x, y) for z, x, y in zip(tn, vp, k_hp)]

    s = [jnp.where(c == 0, s0_ref[d, bl, p], s_scr[bl, d, p]) for bl, d, p in chains]
    s_bd = [bd(x) for x in s]
    y = [_dot_nt(x.astype(BF16), z) + yl for x, z, yl in zip(r_p, s_bd, y_loc)]
    s_new = [x * w + pmm(x, pl_) + qq for x, w, pl_, qq in zip(s, w_p, p_lr, q)]
    for (bl, d, p), yy, ss in zip(chains, y, s_new):
        (yf_ref, yb_ref)[d][bl, :, p * P:(p + 1) * P] = yy
        s_scr[bl, d, p] = ss
        sT_ref[d, bl, p] = ss


def _rwkv_call(r, v, kk, k, lw2, ic2, s0, ka):
    nb, n, w = r.shape
    bt = RWKV_BATCH
    nc = n // RWKV_CHUNK
    items = nb // bt * nc

    def fwd_chunk(item):
        return item % nc

    def bwd_chunk(item):
        return nc - 1 - item % nc

    def item_in(i):
        return jnp.minimum(i, items - 1)

    def item_out(i):
        return jnp.maximum(i - 1, 0)

    def tok(chunk, item):
        return pl.BlockSpec((bt, RWKV_CHUNK, w), lambda i: (item(i) // nc, chunk(item(i)), 0))

    both = [pl.BlockSpec((None, bt, RWKV_CHUNK, w), lambda i: (0, item_in(i) // nc, fwd_chunk(item_in(i)), 0)),
            pl.BlockSpec((None, bt, RWKV_CHUNK, w), lambda i: (1, item_in(i) // nc, bwd_chunk(item_in(i)), 0))]
    st = pl.BlockSpec((2, bt) + RWKV_STATE, lambda i: (0, item_out(i) // nc, 0, 0, 0))
    fwd_in, bwd_in = tok(fwd_chunk, item_in), tok(bwd_chunk, item_in)
    out = jax.ShapeDtypeStruct((nb, n, w), F32)
    return pl.pallas_call(
        functools.partial(_rwkv_kernel, nc=nc),
        grid=(items + 1,),
        in_specs=[fwd_in, fwd_in, fwd_in, bwd_in, bwd_in, bwd_in, fwd_in, bwd_in, *both, *both, st,
                  _resident((1, w))],
        out_specs=[tok(fwd_chunk, item_out), tok(bwd_chunk, item_out), st],
        out_shape=[out, out, jax.ShapeDtypeStruct((2, nb) + RWKV_STATE, F32)],
        scratch_shapes=[pltpu.VMEM((bt, 2) + RWKV_STATE, F32),
                        pltpu.VMEM((2, bt, 2, 3, RWKV_CHUNK, w), F32),
                        pltpu.VMEM((2, bt, 2, 4, RWKV_CHUNK, w), BF16),
                        pltpu.VMEM((2, bt, 2, 1, w), F32)],
        compiler_params=_cparams("arbitrary"),
        name="rwkv",
    )(r, v, kk, r, v, kk, k, k, lw2, lw2, ic2, ic2, s0, ka)


def _out_kernel(x_ref, mod_ref, ya_ref, u_ref, ybf_ref, ybb_ref, bonus_ref, gg_ref, sga_ref, sgb_ref,
                s5d_ref, wglu_ref, wproj_ref, lng_ref, lnb_ref, seg_ref, wo_ref, wout_ref, o_ref, ya_scr):
    nb, tn, _ = x_ref.shape

    per_tile = LANE // S5_GROUP
    for cl in range(tn // S5_CHUNK):
        for q in range(S5_WIDTH // LANE):
            per_g = [ya_ref[q * per_tile + gq, cl * nb:(cl + 1) * nb, :] for gq in range(per_tile)]
            for t in range(S5_CHUNK):
                sl = slice(t * S5_GROUP, (t + 1) * S5_GROUP)
                ya_scr[q, pl.ds(cl * S5_CHUNK + t, nb, stride=tn), :] = jnp.concatenate(
                    [y[:, sl] for y in per_g], axis=1)
    ya_tok = jnp.concatenate([ya_scr[q] for q in range(S5_WIDTH // LANE)], axis=1)
    rows = nb * tn

    def get(ref):
        return ref[...].reshape(rows, ref.shape[-1]).astype(F32)

    ya = jax.nn.gelu(ya_tok + s5d_ref[...] * get(u_ref))
    ya = ya * jax.nn.sigmoid(_dot(ya.astype(BF16), wglu_ref[...]))
    pa = _dot(ya.astype(BF16), wproj_ref[...])

    seg = seg_ref[...]
    yb = get(ybf_ref) + get(ybb_ref)
    mu = _dot_right01(yb, seg) * (1.0 / RWKV_HEAD)
    cen = yb - mu
    var = _dot_right01(cen * cen, seg) * (1.0 / RWKV_HEAD)
    yb = cen * lax.rsqrt(var + LN_X_EPS) * lng_ref[...] + lnb_ref[...] + get(bonus_ref)
    pb = _dot((yb * get(gg_ref)).astype(BF16), wo_ref[...])

    merged = get(sga_ref) * pa + get(sgb_ref) * pb
    out = _dot(merged.astype(BF16), wout_ref[...]).reshape(nb, tn, D_MODEL)
    o_ref[...] = x_ref[...] + mod_ref[:, 5:6] * out


def _out_call(x, mods, ya, u, ybf, ybb, bonus, gg, sga, sgb, s5d, wglu, wproj, lng, lnb, seg, wo, wout):
    nb, n, _ = x.shape
    tn = BATCH_TILE
    w = RWKV_WIDTH

    def tok(width):
        return pl.BlockSpec((nb, tn, width), lambda i: (0, i, 0))

    s5_rows = tn // S5_CHUNK * nb
    return pl.pallas_call(
        _out_kernel,
        grid=(n // tn,),
        in_specs=[tok(D_MODEL), _resident((nb, N_MOD, D_MODEL)),
                  pl.BlockSpec((S5_GROUPS, s5_rows, S5_CHUNK * S5_GROUP), lambda i: (0, i, 0)), tok(w),
                  tok(w), tok(w), tok(w), tok(w), tok(D_MODEL), tok(D_MODEL),
                  _resident((1, w)), _resident((w, w)), _resident((w, D_MODEL)), _resident((1, w)),
                  _resident((1, w)), _resident((w, w)), _resident((w, D_MODEL)), _resident((D_MODEL, D_MODEL))],
        out_specs=tok(D_MODEL),
        out_shape=jax.ShapeDtypeStruct((nb, n, D_MODEL), F32),
        scratch_shapes=[pltpu.VMEM((S5_WIDTH // LANE, nb * tn, LANE), F32)],
        compiler_params=_cparams("parallel"),
        name="outk",
    )(x, mods, ya, u, ybf, ybb, bonus, gg, sga, sgb, s5d, wglu, wproj, lng, lnb, seg, wo, wout)


def _block_diag2(a):
    z = jnp.zeros_like(a[0])
    return jnp.concatenate([jnp.concatenate([a[0], z], axis=1), jnp.concatenate([z, a[1]], axis=1)], axis=0)


def kernel(x, c, ctx, c_ctx, w_mod, b_mod, norm_g, ffn_w_gate, ffn_w_up, ffn_w_down, w_in, s5_A_re, s5_A_im, s5_log_dt, s5_B_re, s5_B_im, s5_C_re, s5_C_im, s5_D, s5_w_glu, s5_w_proj, rwkv_conv, rwkv_w0, rwkv_w2, rwkv_a0, rwkv_a2, rwkv_g2, rwkv_k_k, rwkv_k_a, rwkv_r_k, rwkv_ln_g, rwkv_ln_b, rwkv_w_o, w_out, final_g):
    nb, seq, d = x.shape
    n_ctx = ctx.shape[1]
    l = 0
    w = RWKV_WIDTH

    mod_rows = 16
    cc = jnp.concatenate([c, c_ctx[None], jnp.zeros((mod_rows - nb - 1, d), F32)], axis=0)
    mods = _mod_call(cc, w_mod[l], b_mod[l][None]).reshape(mod_rows, N_MOD, d)

    xt = x.reshape(nb * seq, d)
    ct = ctx.reshape(nb * n_ctx, d)
    row = lambda a: a.reshape(1, -1)
    bf = lambda a: a.astype(BF16)

    f1 = (row(norm_g[l, 0]), bf(ffn_w_gate[l, 0]), bf(ffn_w_up[l, 0]), bf(ffn_w_down[l, 0]), row(final_g))
    x1 = _ffn_call(xt, mods, *f1, j=0, tokens_per_batch=seq, shared_row=None, final_norm=False)
    c1 = _ffn_call(ct, mods, *f1, j=0, tokens_per_batch=n_ctx, shared_row=nb, final_norm=False)

    ip = (row(norm_g[l, 1]), bf(w_in[l]), bf(_block_diag2(rwkv_w2[l])), row(rwkv_w0[l]),
          bf(_block_diag2(rwkv_a2[l])), row(rwkv_a0[l]), bf(rwkv_g2[l]))
    mods_x = mods[:nb]
    mods_c = jnp.broadcast_to(mods[nb:nb + 1], (nb, N_MOD, d))
    u_x, us5_x, rkv_x, lw_x, ic_x, ics_x, gg_x, sga_x, sgb_x = _inproj_call(x1.reshape(nb, seq, d), mods_x, *ip)
    _, us5_c, rkv_c, lw_c, ic_c, ics_c, _, _, _ = _inproj_call(c1.reshape(nb, n_ctx, d), mods_c, *ip)

    def flat(a):
        return a.reshape(a.shape[:-3] + (a.shape[-3] * a.shape[-2], a.shape[-1]))

    s5p = _s5_matrices(s5_A_re[l], s5_A_im[l], s5_log_dt[l], s5_B_re[l], s5_B_im[l], s5_C_re[l], s5_C_im[l], nb)
    ya_s5 = _s5_call(us5_c, us5_x, *s5p, nb=nb)

    seg = (jnp.arange(w)[:, None] // RWKV_HEAD == jnp.arange(w)[None, :] // RWKV_HEAD).astype(BF16)
    pp = (rwkv_conv[l].reshape(9, 3 * w), row(rwkv_k_k[l]), row(rwkv_k_a[l]), row(rwkv_r_k[l]), seg)
    r_c, v_c, kk_c, k_c, _ = _prep_call(flat(rkv_c), flat(ics_c), *pp, tokens_per_batch=n_ctx, width=n_ctx,
                                        vertical=False)
    r_x, v_x, kk_x, k_x, bonus = _prep_call(flat(rkv_x), flat(ics_x), *pp, tokens_per_batch=seq, width=GRID_W,
                                            vertical=True)

    def by_batch(a, n):
        return a.reshape(nb, n, a.shape[-1])

    s0 = jnp.zeros((2, nb) + RWKV_STATE, F32)
    ka = row(rwkv_k_a[l])
    _, _, s_ctx = _rwkv_call(*(by_batch(a, n_ctx) for a in (r_c, v_c, kk_c, k_c)), lw_c, ic_c, s0, ka)
    ybf, ybb, _ = _rwkv_call(*(by_batch(a, seq) for a in (r_x, v_x, kk_x, k_x)), lw_x, ic_x, s_ctx, ka)

    x2 = _out_call(by_batch(x1, seq), mods_x, ya_s5, u_x, ybf, ybb, by_batch(bonus, seq), gg_x, sga_x,
                   sgb_x, row(s5_D[l]), bf(s5_w_glu[l]), bf(s5_w_proj[l]), row(rwkv_ln_g[l]), row(rwkv_ln_b[l]),
                   seg, bf(rwkv_w_o[l]), bf(w_out[l]))

    f2 = (row(norm_g[l, 2]), bf(ffn_w_gate[l, 1]), bf(ffn_w_up[l, 1]), bf(ffn_w_down[l, 1]), row(final_g))
    out = _ffn_call(flat(x2), mods, *f2, j=2, tokens_per_batch=seq, shared_row=None, final_norm=True)
    return out.reshape(nb, seq, d)
```
